```python
import jax, jax.numpy as jnp
from jax import lax
import numpy as np

D_MODEL = 2048
BATCH = 4
SEQ = 4096
DEPTH = 1

RMS_EPS = 1e-5
ROPE_THETA = 10000.0
RET_HEADS = 8
RET_DK = 128
RET_DV = 256
RET_CHUNK = 128
SWA_Q_HEADS = 32
SWA_KV_HEADS = 4
SWA_HEAD_DIM = 64
SWA_WINDOW = 128
SWA_BLOCK = 128
N_EXPERTS = 32
TOP_K = 4
D_FF_EXPERT = 2048
SWIGLU_LIMIT = 7.0
SWIGLU_ALPHA = 1.702

N_BRANCHES = 2
RET_QK_W = RET_HEADS * RET_DK
RET_V_W = RET_HEADS * RET_DV
SWA_Q_W = SWA_Q_HEADS * SWA_HEAD_DIM
SWA_KV_W = SWA_KV_HEADS * SWA_HEAD_DIM
IN_SPLITS = (RET_QK_W, RET_QK_W, RET_V_W, RET_V_W, SWA_Q_W, SWA_KV_W, SWA_KV_W, N_BRANCHES * D_MODEL)
IN_WIDTH = sum(IN_SPLITS)

kernel_name = "hybrid_retention_swa_sink_moe"


def rms_norm(x, w):
    xf = x.astype(jnp.float32)
    y = xf * lax.rsqrt(jnp.mean(xf * xf, axis=-1, keepdims=True) + RMS_EPS)
    return (y * w.astype(jnp.float32)).astype(x.dtype)


def rope_tables(positions, head_dim):
    inv_freq = 1.0 / (ROPE_THETA ** (jnp.arange(0, head_dim, 2, dtype=jnp.float32) / head_dim))
    ang = positions.astype(jnp.float32)[..., None] * inv_freq
    return jnp.cos(ang), jnp.sin(ang)


def apply_rope(x, cos, sin):
    x1, x2 = jnp.split(x.astype(jnp.float32), 2, axis=-1)
    c = cos[:, :, None, :]
    s = sin[:, :, None, :]
    return jnp.concatenate([x1 * c - x2 * s, x2 * c + x1 * s], axis=-1)


def retention(q, k, v, g):
    B, S = q.shape[0], q.shape[1]
    C = RET_CHUNK
    nc = S // C
    H = RET_HEADS
    log_gamma = jnp.log1p(-jnp.exp2(-5.0 - jnp.arange(H, dtype=jnp.float32)))
    qc = q.astype(jnp.float32).reshape(B, nc, C, H, RET_DK)
    kc = (k.astype(jnp.float32) * (RET_DK ** -0.5)).reshape(B, nc, C, H, RET_DK)
    vc = v.astype(jnp.float32).reshape(B, nc, C, H, RET_DV)
    n = jnp.arange(C, dtype=jnp.float32)
    rel = n[:, None] - n[None, :]
    inner_decay = jnp.where(rel >= 0, jnp.exp(log_gamma[:, None, None] * jnp.maximum(rel, 0.0)), 0.0)
    scores = jnp.einsum('bnihd,bnjhd->bhnij', qc, kc) * inner_decay[:, None]
    inner = jnp.einsum('bhnij,bnjhe->bnihe', scores, vc)
    zeta = jnp.exp(log_gamma[:, None] * (C - 1 - n))
    u = jnp.einsum('bnjhd,bnjhe,hj->bnhde', kc, vc, zeta)
    chunk_decay = jnp.exp(log_gamma * C)[None, :, None, None]

    def step(state, u_i):
        return chunk_decay * state + u_i, state

    _, prev = lax.scan(step, jnp.zeros((B, H, RET_DK, RET_DV), jnp.float32), jnp.moveaxis(u, 1, 0))
    prev = jnp.moveaxis(prev, 0, 1)
    xi = jnp.exp(log_gamma[:, None] * (n + 1.0))
    cross = jnp.einsum('bnihd,bnhde,hi->bnihe', qc, prev, xi)
    o = (inner + cross).reshape(B, S, H, RET_DV)
    o = o * lax.rsqrt(jnp.mean(o * o, axis=-1, keepdims=True) + RMS_EPS)
    o = jax.nn.silu(g.astype(jnp.float32)) * o
    return o.reshape(B, S, H * RET_DV)


def sliding_window_attention(q, k, v, sinks):
    B, S = q.shape[0], q.shape[1]
    Wb = SWA_BLOCK
    nb = S // Wb
    G = SWA_Q_HEADS // SWA_KV_HEADS
    qb = q.reshape(B, nb, Wb, SWA_KV_HEADS, G, SWA_HEAD_DIM)

    def banded(t):
        tb = t.astype(jnp.float32).reshape(B, nb, Wb, SWA_KV_HEADS, SWA_HEAD_DIM)
        prev = jnp.concatenate([jnp.zeros_like(tb[:, :1]), tb[:, :-1]], axis=1)
        return jnp.concatenate([prev, tb], axis=2)

    kb, vb = banded(k), banded(v)
    s = jnp.einsum('bnikgd,bnjkd->bkgnij', qb, kb) * (SWA_HEAD_DIM ** -0.5)
    qi = jnp.arange(Wb)[:, None]
    kj = jnp.arange(2 * Wb)[None, :] - Wb
    dist = qi - kj
    in_window = (dist >= 0) & (dist < SWA_WINDOW)
    blk = jnp.arange(nb)[:, None, None]
    valid = in_window[None] & ((blk > 0) | (kj >= 0)[None])
    s = jnp.where(valid, s, -jnp.inf)
    sink = sinks.astype(jnp.float32).reshape(1, SWA_KV_HEADS, G, 1, 1, 1)
    m = jnp.maximum(jnp.max(s, axis=-1, keepdims=True), sink)
    p = jnp.exp(s - m)
    p = p / (jnp.sum(p, axis=-1, keepdims=True) + jnp.exp(sink - m))
    o = jnp.einsum('bkgnij,bnjkd->bnikgd', p, vb)
    return o.reshape(B, S, SWA_Q_HEADS * SWA_HEAD_DIM)


def moe_ffn(x, router_w, router_b, w_gate_up, b_gate_up, w_down, b_down):
    B, S, D = x.shape
    xt = x.reshape(B * S, D)
    logits = (xt @ router_w + router_b).astype(jnp.float32)
    top_vals, top_idx = lax.top_k(logits, TOP_K)
    top_w = jax.nn.softmax(top_vals, axis=-1)
    combine = jnp.einsum('nk,nke->ne', top_w, jax.nn.one_hot(top_idx, N_EXPERTS, dtype=jnp.float32))
    out = jnp.zeros((B * S, D), jnp.float32)
    for e in range(N_EXPERTS):
        hg = (xt @ w_gate_up[e] + b_gate_up[e]).astype(jnp.float32)
        gate = jnp.minimum(hg[:, 0::2], SWIGLU_LIMIT)
        lin = jnp.clip(hg[:, 1::2], -SWIGLU_LIMIT, SWIGLU_LIMIT)
        act = gate * jax.nn.sigmoid(SWIGLU_ALPHA * gate) * (lin + 1.0)
        y = act.astype(x.dtype) @ w_down[e] + b_down[e]
        out = out + combine[:, e:e + 1] * y.astype(jnp.float32)
    return out.reshape(B, S, D).astype(x.dtype)


def setup_inputs(seed: int = 0) -> dict:
    key = jax.random.key(seed)
    ks = jax.random.split(key, 20)
    f32 = jnp.float32
    L = DEPTH

    def nrm(k, shape, scale):
        return jax.random.normal(k, shape, f32) * scale

    x = nrm(ks[0], (BATCH, SEQ, D_MODEL), 1.0)
    offset = jax.random.randint(ks[1], (BATCH, 1), 0, 1024, dtype=jnp.int32)
    positions = offset + jnp.arange(SEQ, dtype=jnp.int32)[None, :]
    return {
        "x": x,
        "positions": positions,
        "attn_norm_w": 1.0 + nrm(ks[2], (L, D_MODEL), 0.02),
        "w_in": nrm(ks[3], (L, D_MODEL, IN_WIDTH), D_MODEL ** -0.5),
        "gate_bias": nrm(ks[4], (L, N_BRANCHES * D_MODEL), 0.02),
        "w_ret_out": nrm(ks[5], (L, RET_V_W, D_MODEL), RET_V_W ** -0.5),
        "w_swa_out": nrm(ks[6], (L, SWA_Q_W, D_MODEL), SWA_Q_W ** -0.5),
        "w_o": nrm(ks[7], (L, D_MODEL, D_MODEL), D_MODEL ** -0.5),
        "sinks": nrm(ks[8], (L, SWA_Q_HEADS), 0.5),
        "ffn_norm_w": 1.0 + nrm(ks[9], (L, D_MODEL), 0.02),
        "router_w": nrm(ks[10], (L, D_MODEL, N_EXPERTS), D_MODEL ** -0.5),
        "router_b": nrm(ks[11], (L, N_EXPERTS), 0.01),
        "w_gate_up": nrm(ks[12], (L, N_EXPERTS, D_MODEL, 2 * D_FF_EXPERT), D_MODEL ** -0.5),
        "b_gate_up": nrm(ks[13], (L, N_EXPERTS, 2 * D_FF_EXPERT), 0.01),
        "w_down": nrm(ks[14], (L, N_EXPERTS, D_FF_EXPERT, D_MODEL), D_FF_EXPERT ** -0.5),
        "b_down": nrm(ks[15], (L, N_EXPERTS, D_MODEL), 0.01),
        "final_norm_w": 1.0 + nrm(ks[16], (D_MODEL,), 0.02),
    }


def reference(x, positions, attn_norm_w, w_in, gate_bias, w_ret_out, w_swa_out, w_o, sinks,
              ffn_norm_w, router_w, router_b, w_gate_up, b_gate_up, w_down, b_down, final_norm_w):
    B, S, D = x.shape
    cos_r, sin_r = rope_tables(positions, RET_DK)
    cos_s, sin_s = rope_tables(positions, SWA_HEAD_DIM)
    split_at = [int(i) for i in np.cumsum(IN_SPLITS)[:-1]]
    h = x
    for l in range(DEPTH):
        xn = rms_norm(h, attn_norm_w[l])
        proj = xn @ w_in[l]
        rq, rk, rv, rg, sq, sk, sv, gl = jnp.split(proj, split_at, axis=-1)
        rq = apply_rope(rq.reshape(B, S, RET_HEADS, RET_DK), cos_r, sin_r)
        rk = apply_rope(rk.reshape(B, S, RET_HEADS, RET_DK), cos_r, sin_r)
        ret = retention(rq, rk, rv.reshape(B, S, RET_HEADS, RET_DV), rg.reshape(B, S, RET_HEADS, RET_DV))
        branch_a = ret.astype(x.dtype) @ w_ret_out[l]
        sq = apply_rope(sq.reshape(B, S, SWA_Q_HEADS, SWA_HEAD_DIM), cos_s, sin_s)
        sk = apply_rope(sk.reshape(B, S, SWA_KV_HEADS, SWA_HEAD_DIM), cos_s, sin_s)
        swa = sliding_window_attention(sq, sk, sv.reshape(B, S, SWA_KV_HEADS, SWA_HEAD_DIM), sinks[l])
        branch_b = swa.astype(x.dtype) @ w_swa_out[l]
        gates = jax.nn.sigmoid((gl + gate_bias[l]).astype(jnp.float32)).reshape(B, S, N_BRANCHES, D)
        mix = gates[:, :, 0] * branch_a.astype(jnp.float32) + gates[:, :, 1] * branch_b.astype(jnp.float32)
        h = h + mix.astype(x.dtype) @ w_o[l]
        h = h + moe_ffn(rms_norm(h, ffn_norm_w[l]), router_w[l], router_b[l],
                        w_gate_up[l], b_gate_up[l], w_down[l], b_down[l])
    return rms_norm(h, final_norm_w)
```

```python
import functools
import math

import jax
import jax.numpy as jnp
from jax import lax
from jax.experimental import pallas as pl
from jax.experimental.pallas import tpu as pltpu

F32 = jnp.float32
BF16 = jnp.bfloat16
I32 = jnp.int32

D_MODEL = 2048
RMS_EPS = 1e-5
ROPE_THETA = 10000.0
RET_HEADS = 8
RET_DK = 128
RET_DV = 256
RET_CHUNK = 128
SWA_Q_HEADS = 32
SWA_KV_HEADS = 4
SWA_HEAD_DIM = 64
SWA_WINDOW = 128
SWA_BLOCK = 128
N_EXPERTS = 32
TOP_K = 4
D_FF = 2048
SWIGLU_LIMIT = 7.0
SWIGLU_ALPHA = 1.702

RET_QK_W = RET_HEADS * RET_DK
RET_V_W = RET_HEADS * RET_DV
SWA_Q_W = SWA_Q_HEADS * SWA_HEAD_DIM
SWA_KV_W = SWA_KV_HEADS * SWA_HEAD_DIM
GATE_W = 2 * D_MODEL
IN_WIDTH = 2 * RET_QK_W + 2 * RET_V_W + SWA_Q_W + 2 * SWA_KV_W + GATE_W

LANES = 128
VMEM_LIMIT = 56 * 1024 * 1024
NEG_BIG = -1e30

_NT = (((1,), (1,)), ((), ()))
_TN = (((0,), (0,)), ((), ()))


def _cparams(sem):
    return pltpu.CompilerParams(dimension_semantics=sem, vmem_limit_bytes=VMEM_LIMIT)


def _sigmoid(x):
    return 1.0 / (1.0 + jnp.exp(-x))


def _rope_table_kernel(pos_ref, invr_ref, invs_ref, cr_ref, sr_ref, cs_ref, sa_ref, sb_ref):
    pos = pos_ref[...]
    lane = lax.broadcasted_iota(I32, (pos.shape[0], LANES), 1)
    ang_r = pos * invr_ref[...]
    cr_ref[...] = jnp.cos(ang_r)
    sr_ref[...] = jnp.where(lane < RET_DK // 2, -1.0, 1.0) * jnp.sin(ang_r)
    ang_s = pos * invs_ref[...]
    sin_s = jnp.sin(ang_s)
    first_half = (lane % SWA_HEAD_DIM) < SWA_HEAD_DIM // 2
    cs_ref[...] = jnp.cos(ang_s)
    sa_ref[...] = jnp.where(first_half, -sin_s, 0.0)
    sb_ref[...] = jnp.where(first_half, 0.0, sin_s)


def _rope_tables(positions):
    n = positions.size
    pos = positions.reshape(n, 1).astype(F32)
    inv_r = 1.0 / (ROPE_THETA ** (jnp.arange(0, RET_DK, 2, dtype=F32) / RET_DK))
    inv_s = 1.0 / (ROPE_THETA ** (jnp.arange(0, SWA_HEAD_DIM, 2, dtype=F32) / SWA_HEAD_DIM))
    inv_r = jnp.tile(inv_r, LANES // inv_r.size).reshape(1, LANES)
    inv_s = jnp.tile(inv_s, LANES // inv_s.size).reshape(1, LANES)
    t = min(2048, n)
    tab = pl.BlockSpec((t, LANES), lambda i: (i, 0))
    row = pl.BlockSpec((1, LANES), lambda i: (0, 0))
    return pl.pallas_call(
        _rope_table_kernel,
        grid=(n // t,),
        in_specs=[pl.BlockSpec((t, 1), lambda i: (i, 0)), row, row],
        out_specs=[tab] * 5,
        out_shape=[jax.ShapeDtypeStruct((n, LANES), F32)] * 5,
        compiler_params=_cparams(("parallel",)),
        name="rope_tables",
    )(pos, inv_r, inv_s)


def _inproj_kernel(x_ref, nw_ref, w_ref, o_ref, xn_ref):
    @pl.when(pl.program_id(1) == 0)
    def _():
        x = x_ref[...]
        ms = jnp.mean(x * x, axis=-1, keepdims=True)
        xn_ref[...] = (x * lax.rsqrt(ms + RMS_EPS) * nw_ref[...]).astype(BF16)

    o_ref[...] = jnp.dot(xn_ref[...], w_ref[...], preferred_element_type=F32).astype(o_ref.dtype)


def _in_projection(x2, norm_w, w_in_bf16):
    n, d = x2.shape
    width = w_in_bf16.shape[1]
    tm = min(1024, n)
    tn = 1280
    return pl.pallas_call(
        _inproj_kernel,
        grid=(n // tm, width // tn),
        in_specs=[pl.BlockSpec((tm, d), lambda i, j: (i, 0)),
                  pl.BlockSpec((1, d), lambda i, j: (0, 0)),
                  pl.BlockSpec((d, tn), lambda i, j: (0, j))],
        out_specs=pl.BlockSpec((tm, tn), lambda i, j: (i, j)),
        out_shape=jax.ShapeDtypeStruct((n, width), BF16),
        scratch_shapes=[pltpu.VMEM((tm, d), BF16)],
        compiler_params=_cparams(("parallel", "arbitrary")),
        name="in_projection",
    )(x2, norm_w.reshape(1, d), w_in_bf16)


def _retention_kernel(q_ref, k_ref, v_ref, g_ref, cr_ref, sr_ref, o_ref, st_ref):
    @pl.when(pl.program_id(1) == 0)
    def _():
        st_ref[...] = jnp.zeros_like(st_ref)

    c = RET_CHUNK
    cr = cr_ref[...]
    sr = sr_ref[...]
    row = lax.broadcasted_iota(I32, (c, c), 0)
    col = lax.broadcasted_iota(I32, (c, c), 1)
    rel = (row - col).astype(F32)
    n_row = row.astype(F32)
    scale = RET_DK ** -0.5
    for h in range(RET_HEADS):
        log_gamma = math.log1p(-(2.0 ** (-5.0 - h)))
        q = q_ref[:, h * RET_DK:(h + 1) * RET_DK].astype(F32)
        k = k_ref[:, h * RET_DK:(h + 1) * RET_DK].astype(F32)
        qr = q * cr + pltpu.roll(q, RET_DK // 2, 1) * sr
        kr = k * cr + pltpu.roll(k, RET_DK // 2, 1) * sr
        decay = jnp.where(rel >= 0, jnp.exp(log_gamma * jnp.maximum(rel, 0.0)), 0.0) * scale
        s = lax.dot_general(qr.astype(BF16), kr.astype(BF16), _NT, preferred_element_type=F32) * decay
        v = v_ref[:, h * RET_DV:(h + 1) * RET_DV]
        inner = jnp.dot(s.astype(BF16), v, preferred_element_type=F32)
        xi = jnp.exp(log_gamma * (n_row + 1.0))
        st = st_ref[h]
        cross = jnp.dot((qr * xi).astype(BF16), st.astype(BF16), preferred_element_type=F32)
        o = inner + cross
        zeta = jnp.exp(log_gamma * (c - 1.0 - n_row)) * scale
        u = lax.dot_general((kr * zeta).astype(BF16), v, _TN, preferred_element_type=F32)
        st_ref[h] = math.exp(log_gamma * c) * st + u
        o = o * lax.rsqrt(jnp.mean(o * o, axis=-1, keepdims=True) + RMS_EPS)
        g = g_ref[:, h * RET_DV:(h + 1) * RET_DV].astype(F32)
        o_ref[:, h * RET_DV:(h + 1) * RET_DV] = (g * _sigmoid(g) * o).astype(o_ref.dtype)


def _retention(proj3, cr3, sr3):
    b, s, _ = proj3.shape
    c = RET_CHUNK
    return pl.pallas_call(
        _retention_kernel,
        grid=(b, s // c),
        in_specs=[pl.BlockSpec((None, c, RET_QK_W), lambda i, j: (i, j, 0)),
                  pl.BlockSpec((None, c, RET_QK_W), lambda i, j: (i, j, 1)),
                  pl.BlockSpec((None, c, RET_V_W), lambda i, j: (i, j, 1)),
                  pl.BlockSpec((None, c, RET_V_W), lambda i, j: (i, j, 2)),
                  pl.BlockSpec((None, c, LANES), lambda i, j: (i, j, 0)),
                  pl.BlockSpec((None, c, LANES), lambda i, j: (i, j, 0))],
        out_specs=pl.BlockSpec((None, c, RET_V_W), lambda i, j: (i, j, 0)),
        out_shape=jax.ShapeDtypeStruct((b, s, RET_V_W), BF16),
        scratch_shapes=[pltpu.VMEM((RET_HEADS, RET_DK, RET_DV), F32)],
        compiler_params=_cparams(("arbitrary", "arbitrary")),
        name="retention",
    )(proj3, proj3, proj3, proj3, cr3, sr3)


def _swa_kernel(sinks_ref, q_ref, kv_ref, kvp_ref, cs_ref, sa_ref, sb_ref, csp_ref, sap_ref, sbp_ref, o_ref):
    blk = pl.program_id(1)
    wb = SWA_BLOCK
    hd = SWA_HEAD_DIM
    group = SWA_Q_HEADS // SWA_KV_HEADS

    def rope(x, cs, sa, sb):
        return x * cs + pltpu.roll(x, LANES - hd // 2, 1) * sa + pltpu.roll(x, hd // 2, 1) * sb

    def rope_keys(ref, tabs):
        return [rope(ref[:, j * LANES:(j + 1) * LANES].astype(F32), *tabs) for j in range(SWA_KV_W // LANES)]

    cur_tabs = (cs_ref[...], sa_ref[...], sb_ref[...])
    kc = rope_keys(kv_ref, cur_tabs)
    kp = rope_keys(kvp_ref, (csp_ref[...], sap_ref[...], sbp_ref[...]))
    vc = kv_ref[:, SWA_KV_W:2 * SWA_KV_W].astype(F32)
    vp = kvp_ref[:, SWA_KV_W:2 * SWA_KV_W].astype(F32)

    qr = [rope(q_ref[:, j * LANES:(j + 1) * LANES].astype(F32), *cur_tabs) * (hd ** -0.5)
          for j in range(SWA_Q_W // LANES)]

    def head_cols(slabs, h):
        return slabs[h // 2][:, (h % 2) * hd:(h % 2 + 1) * hd]

    rows = group * wb
    qi = lax.broadcasted_iota(I32, (rows, 2 * wb), 0) & (wb - 1)
    kj = lax.broadcasted_iota(I32, (rows, 2 * wb), 1)
    valid = (kj > qi) & (kj <= qi + SWA_WINDOW) & ((kj >= wb) | (blk > 0))

    for kh in range(SWA_KV_HEADS):
        kband = jnp.concatenate([head_cols(kp, kh), head_cols(kc, kh)], axis=0).astype(BF16)
        vband = jnp.concatenate([vp[:, kh * hd:(kh + 1) * hd], vc[:, kh * hd:(kh + 1) * hd]], axis=0).astype(BF16)
        heads = [kh * group + g for g in range(group)]
        qg = jnp.concatenate([head_cols(qr, h) for h in heads], axis=0).astype(BF16)
        sink = jnp.concatenate([jnp.full((wb, 1), sinks_ref[h], F32) for h in heads], axis=0)
        s = lax.dot_general(qg, kband, _NT, preferred_element_type=F32)
        s = jnp.where(valid, s, NEG_BIG)
        m = jnp.maximum(jnp.max(s, axis=-1, keepdims=True), sink)
        p = jnp.exp(s - m)
        denom = jnp.sum(p, axis=-1, keepdims=True) + jnp.exp(sink - m)
        o = jnp.dot(p.astype(BF16), vband, preferred_element_type=F32) / denom
        for g in range(0, group, 2):
            pair = jnp.concatenate([o[g * wb:(g + 1) * wb], o[(g + 1) * wb:(g + 2) * wb]], axis=1)
            h = heads[g]
            o_ref[:, h * hd:(h + 2) * hd] = pair.astype(o_ref.dtype)


def _sliding_window(proj3, sinks, cs3, sa3, sb3):
    b, s, _ = proj3.shape
    wb = SWA_BLOCK
    q_blk = (2 * RET_QK_W + 2 * RET_V_W) // SWA_Q_W
    kv_blk = (2 * RET_QK_W + 2 * RET_V_W + SWA_Q_W + GATE_W) // (2 * SWA_KV_W)
    prev = lambda j: jnp.maximum(j - 1, 0)
    tab = pl.BlockSpec((None, wb, LANES), lambda i, j: (i, j, 0))
    tab_prev = pl.BlockSpec((None, wb, LANES), lambda i, j: (i, prev(j), 0))
    return pl.pallas_call(
        _swa_kernel,
        grid=(b, s // wb),
        in_specs=[pl.BlockSpec(memory_space=pltpu.SMEM),
                  pl.BlockSpec((None, wb, SWA_Q_W), lambda i, j: (i, j, q_blk)),
                  pl.BlockSpec((None, wb, 2 * SWA_KV_W), lambda i, j: (i, j, kv_blk)),
                  pl.BlockSpec((None, wb, 2 * SWA_KV_W), lambda i, j: (i, prev(j), kv_blk)),
                  tab, tab, tab, tab_prev, tab_prev, tab_prev],
        out_specs=pl.BlockSpec((None, wb, SWA_Q_W), lambda i, j: (i, j, 0)),
        out_shape=jax.ShapeDtypeStruct((b, s, SWA_Q_W), BF16),
        compiler_params=_cparams(("parallel", "parallel")),
        name="sliding_window",
    )(sinks, proj3, proj3, proj3, cs3, sa3, sb3, cs3, sa3, sb3)


def _post_kernel(ret_ref, swa_ref, gl_ref, x_ref, gb_ref, wro_ref, wso_ref, wo_ref, fw_ref, rwt_ref, rb_ref,
                 h1_ref, xn2_ref, idx_ref, tw_ref, rank_ref, cnt_ref, carry_ref):
    @pl.when(pl.program_id(0) == 0)
    def _():
        carry_ref[...] = jnp.zeros_like(carry_ref)

    d = D_MODEL
    a = jnp.dot(ret_ref[...], wro_ref[...], preferred_element_type=F32)
    b = jnp.dot(swa_ref[...], wso_ref[...], preferred_element_type=F32)
    gl = gl_ref[...].astype(F32) + gb_ref[...]
    mix = _sigmoid(gl[:, :d]) * a + _sigmoid(gl[:, d:]) * b
    h1 = x_ref[...] + jnp.dot(mix.astype(BF16), wo_ref[...], preferred_element_type=F32)
    h1_ref[...] = h1
    xn2 = h1 * lax.rsqrt(jnp.mean(h1 * h1, axis=-1, keepdims=True) + RMS_EPS) * fw_ref[...]
    xn2_ref[...] = xn2

    xh = xn2.astype(BF16)
    xl = (xn2 - xh.astype(F32)).astype(BF16)
    rw = rwt_ref[...]
    rh = rw.astype(BF16)
    rl = (rw - rh.astype(F32)).astype(BF16)
    logits = (lax.dot_general(rh, xh, _NT, preferred_element_type=F32)
              + lax.dot_general(rh, xl, _NT, preferred_element_type=F32)
              + lax.dot_general(rl, xh, _NT, preferred_element_type=F32)) + rb_ref[...]
    tm = logits.shape[1]
    e_iota = lax.broadcasted_iota(I32, (N_EXPERTS, tm), 0)
    vals, idxs = [], []
    for _ in range(TOP_K):
        m = jnp.max(logits, axis=0, keepdims=True)
        ix = jnp.min(jnp.where(logits == m, e_iota, N_EXPERTS), axis=0, keepdims=True)
        vals.append(m)
        idxs.append(ix)
        logits = jnp.where(e_iota == ix, -jnp.inf, logits)
    ex = [jnp.exp(v - vals[0]) for v in vals]
    den = ex[0] + ex[1] + ex[2] + ex[3]
    tw_ref[...] = jnp.concatenate([e / den for e in ex], axis=0)
    idx_ref[...] = jnp.concatenate(idxs, axis=0)

    onehot = jnp.zeros((N_EXPERTS, tm), F32)
    for ix in idxs:
        onehot = onehot + (e_iota == ix).astype(F32)
    earlier = (lax.broadcasted_iota(I32, (tm, tm), 0) < lax.broadcasted_iota(I32, (tm, tm), 1)).astype(BF16)
    prefix = jnp.dot(onehot.astype(BF16), earlier, preferred_element_type=F32) + carry_ref[:, 0:1]
    ranks = [jnp.sum(jnp.where(e_iota == ix, prefix, 0.0), axis=0, keepdims=True) for ix in idxs]
    rank_ref[...] = jnp.concatenate(ranks, axis=0).astype(I32)
    carry = carry_ref[...] + jnp.sum(onehot, axis=1, keepdims=True)
    carry_ref[...] = carry
    cnt_ref[...] = carry


def _post_attention(ret2, swa2, proj2, x2, gate_bias, wro, wso, wo, ffn_w, router_w, router_b):
    n, d = x2.shape
    tm = min(256, n)
    gl_blk = (2 * RET_QK_W + 2 * RET_V_W + SWA_Q_W) // GATE_W
    const = lambda shape: pl.BlockSpec(shape, lambda i: (0,) * len(shape), pipeline_mode=pl.Buffered(1))
    rowblk = lambda w: pl.BlockSpec((tm, w), lambda i: (i, 0))
    tokrow = pl.BlockSpec((TOP_K, tm), lambda i: (0, i))
    return pl.pallas_call(
        _post_kernel,
        grid=(n // tm,),
        in_specs=[rowblk(RET_V_W), rowblk(SWA_Q_W),
                  pl.BlockSpec((tm, GATE_W), lambda i: (i, gl_blk)),
                  rowblk(d), const((1, GATE_W)),
                  const((RET_V_W, d)), const((SWA_Q_W, d)), const((d, d)),
                  const((1, d)), const((N_EXPERTS, d)), const((N_EXPERTS, 1))],
        out_specs=[rowblk(d), rowblk(d), tokrow, tokrow, tokrow,
                   pl.BlockSpec((N_EXPERTS, LANES), lambda i: (0, 0))],
        out_shape=[jax.ShapeDtypeStruct((n, d), F32), jax.ShapeDtypeStruct((n, d), F32),
                   jax.ShapeDtypeStruct((TOP_K, n), I32), jax.ShapeDtypeStruct((TOP_K, n), F32),
                   jax.ShapeDtypeStruct((TOP_K, n), I32), jax.ShapeDtypeStruct((N_EXPERTS, LANES), F32)],
        scratch_shapes=[pltpu.VMEM((N_EXPERTS, LANES), F32)],
        compiler_params=_cparams(("arbitrary",)),
        name="post_attention",
    )(ret2, swa2, proj2, x2, gate_bias.reshape(1, GATE_W), wro, wso, wo, ffn_w.reshape(1, d),
      router_w.T, router_b.reshape(N_EXPERTS, 1))


def _dest_kernel(idx_ref, rank_ref, gs_ref, o_ref):
    t = idx_ref.shape[1]
    e_iota = lax.broadcasted_iota(I32, (N_EXPERTS, t), 0)
    gs = gs_ref[...]
    rows = []
    for k in range(TOP_K):
        start = jnp.sum(jnp.where(e_iota == idx_ref[k:k + 1, :], gs, 0), axis=0, keepdims=True)
        rows.append(start + rank_ref[k:k + 1, :])
    o_ref[...] = jnp.concatenate(rows, axis=0)


def _dest_rows(top_idx, rank, group_start):
    n = top_idx.shape[1]
    t = min(2048, n)
    blk = pl.BlockSpec((TOP_K, t), lambda i: (0, i))
    return pl.pallas_call(
        _dest_kernel,
        grid=(n // t,),
        in_specs=[blk, blk, pl.BlockSpec((N_EXPERTS, 1), lambda i: (0, 0))],
        out_specs=blk,
        out_shape=jax.ShapeDtypeStruct((TOP_K, n), I32),
        compiler_params=_cparams(("parallel",)),
        name="dest_rows",
    )(top_idx, rank, group_start.reshape(N_EXPERTS, 1))


def _row_copy(src, s, dst, d, sem):
    return pltpu.make_async_copy(src.at[pl.ds(s, 1), :], dst.at[pl.ds(d, 1), :], sem)


def _dispatch_kernel(dest_ref, x_ref, init_ref, o_ref, sem):
    del init_ref
    i = pl.program_id(0)
    tq = x_ref.shape[0]
    n = pl.num_programs(0) * tq

    def issue(r, carry):
        for k in range(TOP_K):
            _row_copy(x_ref, r, o_ref, dest_ref[k * n + i * tq + r], sem).start()
        return carry

    lax.fori_loop(0, tq, issue, 0)

    def drain(r, carry):
        _row_copy(x_ref, 0, o_ref, 0, sem).wait()
        return carry

    lax.fori_loop(0, TOP_K * tq, drain, 0)


def _dispatch(dest_flat, xn2, n_rows):
    n, d = xn2.shape
    tq = min(256, n)
    return pl.pallas_call(
        _dispatch_kernel,
        grid_spec=pltpu.PrefetchScalarGridSpec(
            num_scalar_prefetch=1,
            grid=(n // tq,),
            in_specs=[pl.BlockSpec((tq, d), lambda i, dest: (i, 0)),
                      pl.BlockSpec(memory_space=pl.ANY)],
            out_specs=pl.BlockSpec(memory_space=pl.ANY),
            scratch_shapes=[pltpu.SemaphoreType.DMA(())],
        ),
        out_shape=jax.ShapeDtypeStruct((n_rows, d), F32),
        input_output_aliases={2: 0},
        compiler_params=_cparams(("arbitrary",)),
        name="dispatch_rows",
    )(dest_flat, xn2, jnp.zeros((n_rows, d), F32))


def _gate_up_kernel(te_ref, na_ref, x_ref, wg_ref, wl_ref, bg_ref, bl_ref, o_ref):
    del te_ref
    active = pl.program_id(0) < na_ref[0]

    @pl.when(active)
    def _():
        x = x_ref[...].astype(BF16)
        fc = 512
        for c in range(D_FF // fc):
            sl = slice(c * fc, (c + 1) * fc)
            gate = jnp.dot(x, wg_ref[:, sl], preferred_element_type=F32) + bg_ref[:, sl]
            lin = jnp.dot(x, wl_ref[:, sl], preferred_element_type=F32) + bl_ref[:, sl]
            gate = jnp.minimum(gate, SWIGLU_LIMIT)
            lin = jnp.clip(lin, -SWIGLU_LIMIT, SWIGLU_LIMIT)
            o_ref[:, sl] = (gate * _sigmoid(SWIGLU_ALPHA * gate) * (lin + 1.0)).astype(o_ref.dtype)

    @pl.when(jnp.logical_not(active))
    def _():
        o_ref[...] = jnp.zeros_like(o_ref)


def _down_kernel(te_ref, na_ref, a_ref, wd_ref, bd_ref, o_ref):
    del te_ref
    active = pl.program_id(0) < na_ref[0]

    @pl.when(active)
    def _():
        o_ref[...] = jnp.dot(a_ref[...], wd_ref[...], preferred_element_type=F32) + bd_ref[...]

    @pl.when(jnp.logical_not(active))
    def _():
        o_ref[...] = jnp.zeros_like(o_ref)


def _expert_ffn(tile_expert, n_active, xs, wg, wl, bg, bl, wd, bd, tm):
    n_rows, d = xs.shape
    n_tiles = n_rows // tm
    f = wg.shape[2]
    row_in = lambda w: pl.BlockSpec((tm, w), lambda i, te, na: (jnp.minimum(i, na[0] - 1), 0))
    row_out = lambda w: pl.BlockSpec((tm, w), lambda i, te, na: (i, 0))
    per_expert = lambda a, b: pl.BlockSpec((None, a, b), lambda i, te, na: (te[i], 0, 0))
    act = pl.pallas_call(
        _gate_up_kernel,
        grid_spec=pltpu.PrefetchScalarGridSpec(
            num_scalar_prefetch=2,
            grid=(n_tiles,),
            in_specs=[row_in(d), per_expert(d, f), per_expert(d, f), per_expert(1, f), per_expert(1, f)],
            out_specs=row_out(f),
        ),
        out_shape=jax.ShapeDtypeStruct((n_rows, f), BF16),
        compiler_params=_cparams(("arbitrary",)),
        name="expert_gate_up",
    )(tile_expert, n_active, xs, wg, wl, bg, bl)
    return pl.pallas_call(
        _down_kernel,
        grid_spec=pltpu.PrefetchScalarGridSpec(
            num_scalar_prefetch=2,
            grid=(n_tiles,),
            in_specs=[row_in(f), per_expert(f, d), per_expert(1, d)],
            out_specs=row_out(d),
        ),
        out_shape=jax.ShapeDtypeStruct((n_rows, d), F32),
        compiler_params=_cparams(("arbitrary",)),
        name="expert_down",
    )(tile_expert, n_active, act, wd, bd)


def _combine_kernel(dest_ref, y_ref, tw_ref, h1_ref, fw_ref, o_ref, buf_ref, sem, *, final_norm):
    i = pl.program_id(0)
    steps = pl.num_programs(0)
    tq = h1_ref.shape[0]
    n = steps * tq

    def issue(step, slot):
        def body(r, carry):
            for k in range(TOP_K):
                src = dest_ref[k * n + step * tq + r]
                pltpu.make_async_copy(y_ref.at[pl.ds(src, 1), :], buf_ref.at[slot, k, pl.ds(r, 1), :],
                                      sem.at[slot]).start()
            return carry
        lax.fori_loop(0, tq, body, 0)

    @pl.when(i == 0)
    def _():
        issue(0, 0)

    @pl.when(i + 1 < steps)
    def _():
        issue(i + 1, (i + 1) % 2)

    slot = i % 2

    def drain(r, carry):
        pltpu.make_async_copy(y_ref.at[pl.ds(0, 1), :], buf_ref.at[slot, 0, pl.ds(0, 1), :], sem.at[slot]).wait()
        return carry

    lax.fori_loop(0, TOP_K * tq, drain, 0)

    acc = h1_ref[...]
    moe = tw_ref[:, 0:1] * buf_ref[slot, 0]
    for k in range(1, TOP_K):
        moe = moe + tw_ref[:, k:k + 1] * buf_ref[slot, k]
    acc = acc + moe
    if final_norm:
        acc = acc * lax.rsqrt(jnp.mean(acc * acc, axis=-1, keepdims=True) + RMS_EPS) * fw_ref[...]
    o_ref[...] = acc


def _combine(dest_flat, y, top_w_t, h1, final_w, final_norm):
    n, d = h1.shape
    tq = min(256, n)
    return pl.pallas_call(
        functools.partial(_combine_kernel, final_norm=final_norm),
        grid_spec=pltpu.PrefetchScalarGridSpec(
            num_scalar_prefetch=1,
            grid=(n // tq,),
            in_specs=[pl.BlockSpec(memory_space=pl.ANY),
                      pl.BlockSpec((tq, TOP_K), lambda i, dest: (i, 0)),
                      pl.BlockSpec((tq, d), lambda i, dest: (i, 0)),
                      pl.BlockSpec((1, d), lambda i, dest: (0, 0))],
            out_specs=pl.BlockSpec((tq, d), lambda i, dest: (i, 0)),
            scratch_shapes=[pltpu.VMEM((2, TOP_K, tq, d), F32), pltpu.SemaphoreType.DMA((2,))],
        ),
        out_shape=jax.ShapeDtypeStruct((n, d), F32),
        compiler_params=_cparams(("arbitrary",)),
        name="combine_rows",
    )(dest_flat, y, top_w_t, h1, final_w.reshape(1, d))


def _reorder_in_columns(w):
    kv0 = 2 * RET_QK_W + 2 * RET_V_W + SWA_Q_W
    kv1 = kv0 + 2 * SWA_KV_W
    return jnp.concatenate([w[:, :kv0], w[:, kv1:], w[:, kv0:kv1]], axis=1)


def _layer(h, tables, p, final_w, final_norm, expert_tile):
    b, s, d = h.shape
    n = b * s
    cr, sr, cs, sa, sb = tables
    x2 = h.reshape(n, d)
    proj = _in_projection(x2, p["attn_norm_w"], _reorder_in_columns(p["w_in"]).astype(BF16))
    proj3 = proj.reshape(b, s, IN_WIDTH)
    t3 = lambda t: t.reshape(b, s, LANES)
    ret = _retention(proj3, t3(cr), t3(sr))
    swa = _sliding_window(proj3, p["sinks"], t3(cs), t3(sa), t3(sb))
    h1, xn2, top_idx, top_w, rank, counts = _post_attention(
        ret.reshape(n, RET_V_W), swa.reshape(n, SWA_Q_W), proj, x2, p["gate_bias"],
        p["w_ret_out"].astype(BF16), p["w_swa_out"].astype(BF16), p["w_o"].astype(BF16),
        p["ffn_norm_w"], p["router_w"], p["router_b"])

    tm = expert_tile
    n_tiles = (n * TOP_K) // tm + N_EXPERTS
    cnt = counts[:, 0].astype(I32)
    padded = ((cnt + tm - 1) // tm) * tm
    ends = jnp.cumsum(padded)
    group_start = ends - padded
    n_active = (ends[-1] // tm).reshape(1)
    tile_ids = jnp.arange(n_tiles, dtype=I32)
    tile_expert = jnp.minimum(
        jnp.sum((tile_ids[:, None] >= (ends // tm)[None, :]).astype(I32), axis=1), N_EXPERTS - 1)

    dest_flat = _dest_rows(top_idx, rank, group_start).reshape(TOP_K * n)
    xs = _dispatch(dest_flat, xn2, n_tiles * tm)

    wgu = p["w_gate_up"]
    wg = wgu[:, :, 0::2].astype(BF16)
    wl = wgu[:, :, 1::2].astype(BF16)
    bgu = p["b_gate_up"]
    bg = bgu[:, 0::2].reshape(N_EXPERTS, 1, D_FF)
    bl = bgu[:, 1::2].reshape(N_EXPERTS, 1, D_FF)
    y = _expert_ffn(tile_expert, n_active, xs, wg, wl, bg, bl, p["w_down"].astype(BF16),
                    p["b_down"].reshape(N_EXPERTS, 1, d), tm)
    out = _combine(dest_flat, y, top_w.T, h1, final_w, final_norm)
    return out.reshape(b, s, d)


def kernel(x, positions, attn_norm_w, w_in, gate_bias, w_ret_out, w_swa_out, w_o, sinks, ffn_norm_w, router_w,
           router_b, w_gate_up, b_gate_up, w_down, b_down, final_norm_w):
    depth = w_in.shape[0]
    stacked = dict(attn_norm_w=attn_norm_w, w_in=w_in, gate_bias=gate_bias, w_ret_out=w_ret_out,
                   w_swa_out=w_swa_out, w_o=w_o, sinks=sinks, ffn_norm_w=ffn_norm_w, router_w=router_w,
                   router_b=router_b, w_gate_up=w_gate_up, b_gate_up=b_gate_up, w_down=w_down, b_down=b_down)
    tables = _rope_tables(positions)
    h = x
    for layer in range(depth):
        p = {name: w[layer] for name, w in stacked.items()}
        h = _layer(h, tables, p, final_norm_w, layer == depth - 1, expert_tile=256)
    return h
```

```python
import functools
import math

import jax
import jax.numpy as jnp
from jax import lax
from jax.experimental import pallas as pl
from jax.experimental.pallas import tpu as pltpu

F32 = jnp.float32
BF16 = jnp.bfloat16
I32 = jnp.int32

D_MODEL = 2048
RMS_EPS = 1e-5
ROPE_THETA = 10000.0
RET_HEADS = 8
RET_DK = 128
RET_DV = 256
RET_CHUNK = 128
SWA_Q_HEADS = 32
SWA_KV_HEADS = 4
SWA_HEAD_DIM = 64
SWA_WINDOW = 128
SWA_BLOCK = 128
N_EXPERTS = 32
TOP_K = 4
D_FF = 2048
SWIGLU_LIMIT = 7.0
SWIGLU_ALPHA = 1.702

RET_QK_W = RET_HEADS * RET_DK
RET_V_W = RET_HEADS * RET_DV
SWA_Q_W = SWA_Q_HEADS * SWA_HEAD_DIM
SWA_KV_W = SWA_KV_HEADS * SWA_HEAD_DIM
GATE_W = 2 * D_MODEL
IN_WIDTH = 2 * RET_QK_W + 2 * RET_V_W + SWA_Q_W + 2 * SWA_KV_W + GATE_W

LANES = 128
VMEM_LIMIT = 56 * 1024 * 1024
NEG_BIG = -1e30

_NT = (((1,), (1,)), ((), ()))
_TN = (((0,), (0,)), ((), ()))


def _cparams(sem):
    return pltpu.CompilerParams(dimension_semantics=sem, vmem_limit_bytes=VMEM_LIMIT)


def _sigmoid(x):
    return 1.0 / (1.0 + jnp.exp(-x))


def _rope_table_kernel(pos_ref, invr_ref, invs_ref, cr_ref, sr_ref, cs_ref, sa_ref, sb_ref):
    pos = pos_ref[...]
    lane = lax.broadcasted_iota(I32, (pos.shape[0], LANES), 1)
    ang_r = pos * invr_ref[...]
    cr_ref[...] = jnp.cos(ang_r)
    sr_ref[...] = jnp.where(lane < RET_DK // 2, -1.0, 1.0) * jnp.sin(ang_r)
    ang_s = pos * invs_ref[...]
    sin_s = jnp.sin(ang_s)
    first_half = (lane % SWA_HEAD_DIM) < SWA_HEAD_DIM // 2
    cs_ref[...] = jnp.cos(ang_s)
    sa_ref[...] = jnp.where(first_half, -sin_s, 0.0)
    sb_ref[...] = jnp.where(first_half, 0.0, sin_s)


def _rope_tables(positions):
    n = positions.size
    pos = positions.reshape(n, 1).astype(F32)
    inv_r = 1.0 / (ROPE_THETA ** (jnp.arange(0, RET_DK, 2, dtype=F32) / RET_DK))
    inv_s = 1.0 / (ROPE_THETA ** (jnp.arange(0, SWA_HEAD_DIM, 2, dtype=F32) / SWA_HEAD_DIM))
    inv_r = jnp.tile(inv_r, LANES // inv_r.size).reshape(1, LANES)
    inv_s = jnp.tile(inv_s, LANES // inv_s.size).reshape(1, LANES)
    t = min(2048, n)
    tab = pl.BlockSpec((t, LANES), lambda i: (i, 0))
    row = pl.BlockSpec((1, LANES), lambda i: (0, 0))
    return pl.pallas_call(
        _rope_table_kernel,
        grid=(n // t,),
        in_specs=[pl.BlockSpec((t, 1), lambda i: (i, 0)), row, row],
        out_specs=[tab] * 5,
        out_shape=[jax.ShapeDtypeStruct((n, LANES), F32)] * 5,
        compiler_params=_cparams(("parallel",)),
        name="rope_tables",
    )(pos, inv_r, inv_s)


def _inproj_kernel(x_ref, nw_ref, w_ref, o_ref, xn_ref):
    @pl.when(pl.program_id(1) == 0)
    def _():
        x = x_ref[...]
        ms = jnp.mean(x * x, axis=-1, keepdims=True)
        xn_ref[...] = (x * lax.rsqrt(ms + RMS_EPS) * nw_ref[...]).astype(BF16)

    o_ref[...] = jnp.dot(xn_ref[...], w_ref[...], preferred_element_type=F32).astype(o_ref.dtype)


def _in_projection(x2, norm_w, w_in_bf16):
    n, d = x2.shape
    width = w_in_bf16.shape[1]
    tm = min(1024, n)
    tn = 1280
    return pl.pallas_call(
        _inproj_kernel,
        grid=(n // tm, width // tn),
        in_specs=[pl.BlockSpec((tm, d), lambda i, j: (i, 0)),
                  pl.BlockSpec((1, d), lambda i, j: (0, 0)),
                  pl.BlockSpec((d, tn), lambda i, j: (0, j))],
        out_specs=pl.BlockSpec((tm, tn), lambda i, j: (i, j)),
        out_shape=jax.ShapeDtypeStruct((n, width), BF16),
        scratch_shapes=[pltpu.VMEM((tm, d), BF16)],
        compiler_params=_cparams(("parallel", "arbitrary")),
        name="in_projection",
    )(x2, norm_w.reshape(1, d), w_in_bf16)


def _retention_kernel(q_ref, k_ref, v_ref, g_ref, cr_ref, sr_ref, o_ref, st_ref):
    @pl.when(pl.program_id(1) == 0)
    def _():
        st_ref[...] = jnp.zeros_like(st_ref)

    c = RET_CHUNK
    cr = cr_ref[...]
    sr = sr_ref[...]
    row = lax.broadcasted_iota(I32, (c, c), 0)
    col = lax.broadcasted_iota(I32, (c, c), 1)
    rel = (row - col).astype(F32)
    n_row = row.astype(F32)
    scale = RET_DK ** -0.5
    for h in range(RET_HEADS):
        log_gamma = math.log1p(-(2.0 ** (-5.0 - h)))
        q = q_ref[:, h * RET_DK:(h + 1) * RET_DK].astype(F32)
        k = k_ref[:, h * RET_DK:(h + 1) * RET_DK].astype(F32)
        qr = q * cr + pltpu.roll(q, RET_DK // 2, 1) * sr
        kr = k * cr + pltpu.roll(k, RET_DK // 2, 1) * sr
        decay = jnp.where(rel >= 0, jnp.exp(log_gamma * jnp.maximum(rel, 0.0)), 0.0) * scale
        s = lax.dot_general(qr.astype(BF16), kr.astype(BF16), _NT, preferred_element_type=F32) * decay
        v = v_ref[:, h * RET_DV:(h + 1) * RET_DV]
        inner = jnp.dot(s.astype(BF16), v, preferred_element_type=F32)
        xi = jnp.exp(log_gamma * (n_row + 1.0))
        st = st_ref[h]
        cross = jnp.dot((qr * xi).astype(BF16), st.astype(BF16), preferred_element_type=F32)
        o = inner + cross
        zeta = jnp.exp(log_gamma * (c - 1.0 - n_row)) * scale
        u = lax.dot_general((kr * zeta).astype(BF16), v, _TN, preferred_element_type=F32)
        st_ref[h] = math.exp(log_gamma * c) * st + u
        o = o * lax.rsqrt(jnp.mean(o * o, axis=-1, keepdims=True) + RMS_EPS)
        g = g_ref[:, h * RET_DV:(h + 1) * RET_DV].astype(F32)
        o_ref[:, h * RET_DV:(h + 1) * RET_DV] = (g * _sigmoid(g) * o).astype(o_ref.dtype)


def _retention(proj3, cr3, sr3):
    b, s, _ = proj3.shape
    c = RET_CHUNK
    return pl.pallas_call(
        _retention_kernel,
        grid=(b, s // c),
        in_specs=[pl.BlockSpec((None, c, RET_QK_W), lambda i, j: (i, j, 0)),
                  pl.BlockSpec((None, c, RET_QK_W), lambda i, j: (i, j, 1)),
                  pl.BlockSpec((None, c, RET_V_W), lambda i, j: (i, j, 1)),
                  pl.BlockSpec((None, c, RET_V_W), lambda i, j: (i, j, 2)),
                  pl.BlockSpec((None, c, LANES), lambda i, j: (i, j, 0)),
                  pl.BlockSpec((None, c, LANES), lambda i, j: (i, j, 0))],
        out_specs=pl.BlockSpec((None, c, RET_V_W), lambda i, j: (i, j, 0)),
        out_shape=jax.ShapeDtypeStruct((b, s, RET_V_W), BF16),
        scratch_shapes=[pltpu.VMEM((RET_HEADS, RET_DK, RET_DV), F32)],
        compiler_params=_cparams(("arbitrary", "arbitrary")),
        name="retention",
    )(proj3, proj3, proj3, proj3, cr3, sr3)


def _swa_kernel(sinks_ref, q_ref, kv_ref, kvp_ref, cs_ref, sa_ref, sb_ref, csp_ref, sap_ref, sbp_ref, o_ref):
    blk = pl.program_id(1)
    wb = SWA_BLOCK
    hd = SWA_HEAD_DIM
    group = SWA_Q_HEADS // SWA_KV_HEADS

    def rope(x, cs, sa, sb):
        return x * cs + pltpu.roll(x, LANES - hd // 2, 1) * sa + pltpu.roll(x, hd // 2, 1) * sb

    def rope_keys(ref, tabs):
        return [rope(ref[:, j * LANES:(j + 1) * LANES].astype(F32), *tabs) for j in range(SWA_KV_W // LANES)]

    cur_tabs = (cs_ref[...], sa_ref[...], sb_ref[...])
    kc = rope_keys(kv_ref, cur_tabs)
    kp = rope_keys(kvp_ref, (csp_ref[...], sap_ref[...], sbp_ref[...]))
    vc = kv_ref[:, SWA_KV_W:2 * SWA_KV_W].astype(F32)
    vp = kvp_ref[:, SWA_KV_W:2 * SWA_KV_W].astype(F32)

    qr = [rope(q_ref[:, j * LANES:(j + 1) * LANES].astype(F32), *cur_tabs) * (hd ** -0.5)
          for j in range(SWA_Q_W // LANES)]

    def head_cols(slabs, h):
        return slabs[h // 2][:, (h % 2) * hd:(h % 2 + 1) * hd]

    rows = group * wb
    qi = lax.broadcasted_iota(I32, (rows, 2 * wb), 0) & (wb - 1)
    kj = lax.broadcasted_iota(I32, (rows, 2 * wb), 1)
    valid = (kj > qi) & (kj <= qi + SWA_WINDOW) & ((kj >= wb) | (blk > 0))

    for kh in range(SWA_KV_HEADS):
        kband = jnp.concatenate([head_cols(kp, kh), head_cols(kc, kh)], axis=0).astype(BF16)
        vband = jnp.concatenate([vp[:, kh * hd:(kh + 1) * hd], vc[:, kh * hd:(kh + 1) * hd]], axis=0).astype(BF16)
        heads = [kh * group + g for g in range(group)]
        qg = jnp.concatenate([head_cols(qr, h) for h in heads], axis=0).astype(BF16)
        sink = jnp.concatenate([jnp.full((wb, 1), sinks_ref[h], F32) for h in heads], axis=0)
        s = lax.dot_general(qg, kband, _NT, preferred_element_type=F32)
        s = jnp.where(valid, s, NEG_BIG)
        m = jnp.maximum(jnp.max(s, axis=-1, keepdims=True), sink)
        p = jnp.exp(s - m)
        denom = jnp.sum(p, axis=-1, keepdims=True) + jnp.exp(sink - m)
        o = jnp.dot(p.astype(BF16), vband, preferred_element_type=F32) / denom
        for g in range(0, group, 2):
            pair = jnp.concatenate([o[g * wb:(g + 1) * wb], o[(g + 1) * wb:(g + 2) * wb]], axis=1)
            h = heads[g]
            o_ref[:, h * hd:(h + 2) * hd] = pair.astype(o_ref.dtype)


def _sliding_window(proj3, sinks, cs3, sa3, sb3):
    b, s, _ = proj3.shape
    wb = SWA_BLOCK
    q_blk = (2 * RET_QK_W + 2 * RET_V_W) // SWA_Q_W
    kv_blk = (2 * RET_QK_W + 2 * RET_V_W + SWA_Q_W + GATE_W) // (2 * SWA_KV_W)
    prev = lambda j: jnp.maximum(j - 1, 0)
    tab = pl.BlockSpec((None, wb, LANES), lambda i, j: (i, j, 0))
    tab_prev = pl.BlockSpec((None, wb, LANES), lambda i, j: (i, prev(j), 0))
    return pl.pallas_call(
        _swa_kernel,
        grid=(b, s // wb),
        in_specs=[pl.BlockSpec(memory_space=pltpu.SMEM),
                  pl.BlockSpec((None, wb, SWA_Q_W), lambda i, j: (i, j, q_blk)),
                  pl.BlockSpec((None, wb, 2 * SWA_KV_W), lambda i, j: (i, j, kv_blk)),
                  pl.BlockSpec((None, wb, 2 * SWA_KV_W), lambda i, j: (i, prev(j), kv_blk)),
                  tab, tab, tab, tab_prev, tab_prev, tab_prev],
        out_specs=pl.BlockSpec((None, wb, SWA_Q_W), lambda i, j: (i, j, 0)),
        out_shape=jax.ShapeDtypeStruct((b, s, SWA_Q_W), BF16),
        compiler_params=_cparams(("parallel", "parallel")),
        name="sliding_window",
    )(sinks, proj3, proj3, proj3, cs3, sa3, sb3, cs3, sa3, sb3)


def _post_kernel(ret_ref, swa_ref, gl_ref, x_ref, gb_ref, wro_ref, wso_ref, wo_ref, fw_ref, rwt_ref, rb_ref,
                 h1_ref, xn2_ref, idx_ref, tw_ref, rank_ref, cnt_ref, carry_ref):
    @pl.when(pl.program_id(0) == 0)
    def _():
        carry_ref[...] = jnp.zeros_like(carry_ref)

    d = D_MODEL
    a = jnp.dot(ret_ref[...], wro_ref[...], preferred_element_type=F32)
    b = jnp.dot(swa_ref[...], wso_ref[...], preferred_element_type=F32)
    gl = gl_ref[...].astype(F32) + gb_ref[...]
    mix = _sigmoid(gl[:, :d]) * a + _sigmoid(gl[:, d:]) * b
    h1 = x_ref[...] + jnp.dot(mix.astype(BF16), wo_ref[...], preferred_element_type=F32)
    h1_ref[...] = h1
    xn2 = h1 * lax.rsqrt(jnp.mean(h1 * h1, axis=-1, keepdims=True) + RMS_EPS) * fw_ref[...]
    xn2_ref[...] = xn2

    xh = xn2.astype(BF16)
    xl = (xn2 - xh.astype(F32)).astype(BF16)
    rw = rwt_ref[...]
    rh = rw.astype(BF16)
    rl = (rw - rh.astype(F32)).astype(BF16)
    logits = (lax.dot_general(rh, xh, _NT, preferred_element_type=F32)
              + lax.dot_general(rh, xl, _NT, preferred_element_type=F32)
              + lax.dot_general(rl, xh, _NT, preferred_element_type=F32)) + rb_ref[...]
    tm = logits.shape[1]
    e_iota = lax.broadcasted_iota(I32, (N_EXPERTS, tm), 0)
    vals, idxs = [], []
    for _ in range(TOP_K):
        m = jnp.max(logits, axis=0, keepdims=True)
        ix = jnp.min(jnp.where(logits == m, e_iota, N_EXPERTS), axis=0, keepdims=True)
        vals.append(m)
        idxs.append(ix)
        logits = jnp.where(e_iota == ix, -jnp.inf, logits)
    ex = [jnp.exp(v - vals[0]) for v in vals]
    den = ex[0] + ex[1] + ex[2] + ex[3]
    tw_ref[...] = jnp.concatenate([e / den for e in ex], axis=0)
    idx_ref[...] = jnp.concatenate(idxs, axis=0)

    onehot = jnp.zeros((N_EXPERTS, tm), F32)
    for ix in idxs:
        onehot = onehot + (e_iota == ix).astype(F32)
    earlier = (lax.broadcasted_iota(I32, (tm, tm), 0) < lax.broadcasted_iota(I32, (tm, tm), 1)).astype(BF16)
    prefix = jnp.dot(onehot.astype(BF16), earlier, preferred_element_type=F32) + carry_ref[:, 0:1]
    ranks = [jnp.sum(jnp.where(e_iota == ix, prefix, 0.0), axis=0, keepdims=True) for ix in idxs]
    rank_ref[...] = jnp.concatenate(ranks, axis=0).astype(I32)
    carry = carry_ref[...] + jnp.sum(onehot, axis=1, keepdims=True)
    carry_ref[...] = carry
    cnt_ref[...] = carry


def _post_attention(ret2, swa2, proj2, x2, gate_bias, wro, wso, wo, ffn_w, router_w, router_b):
    n, d = x2.shape
    tm = min(256, n)
    gl_blk = (2 * RET_QK_W + 2 * RET_V_W + SWA_Q_W) // GATE_W
    const = lambda shape: pl.BlockSpec(shape, lambda i: (0,) * len(shape), pipeline_mode=pl.Buffered(1))
    rowblk = lambda w: pl.BlockSpec((tm, w), lambda i: (i, 0))
    tokrow = pl.BlockSpec((TOP_K, tm), lambda i: (0, i))
    return pl.pallas_call(
        _post_kernel,
        grid=(n // tm,),
        in_specs=[rowblk(RET_V_W), rowblk(SWA_Q_W),
                  pl.BlockSpec((tm, GATE_W), lambda i: (i, gl_blk)),
                  rowblk(d), const((1, GATE_W)),
                  const((RET_V_W, d)), const((SWA_Q_W, d)), const((d, d)),
                  const((1, d)), const((N_EXPERTS, d)), const((N_EXPERTS, 1))],
        out_specs=[rowblk(d), rowblk(d), tokrow, tokrow, tokrow,
                   pl.BlockSpec((N_EXPERTS, LANES), lambda i: (0, 0))],
        out_shape=[jax.ShapeDtypeStruct((n, d), F32), jax.ShapeDtypeStruct((n, d), F32),
                   jax.ShapeDtypeStruct((TOP_K, n), I32), jax.ShapeDtypeStruct((TOP_K, n), F32),
                   jax.ShapeDtypeStruct((TOP_K, n), I32), jax.ShapeDtypeStruct((N_EXPERTS, LANES), F32)],
        scratch_shapes=[pltpu.VMEM((N_EXPERTS, LANES), F32)],
        compiler_params=_cparams(("arbitrary",)),
        name="post_attention",
    )(ret2, swa2, proj2, x2, gate_bias.reshape(1, GATE_W), wro, wso, wo, ffn_w.reshape(1, d),
      router_w.T, router_b.reshape(N_EXPERTS, 1))


def _dest_kernel(idx_ref, rank_ref, gs_ref, o_ref):
    t = idx_ref.shape[1]
    e_iota = lax.broadcasted_iota(I32, (N_EXPERTS, t), 0)
    gs = gs_ref[...]
    rows = []
    for k in range(TOP_K):
        start = jnp.sum(jnp.where(e_iota == idx_ref[k:k + 1, :], gs, 0), axis=0, keepdims=True)
        rows.append(start + rank_ref[k:k + 1, :])
    o_ref[...] = jnp.concatenate(rows, axis=0)


def _dest_rows(top_idx, rank, group_start):
    n = top_idx.shape[1]
    t = min(2048, n)
    blk = pl.BlockSpec((TOP_K, t), lambda i: (0, i))
    return pl.pallas_call(
        _dest_kernel,
        grid=(n // t,),
        in_specs=[blk, blk, pl.BlockSpec((N_EXPERTS, 1), lambda i: (0, 0))],
        out_specs=blk,
        out_shape=jax.ShapeDtypeStruct((TOP_K, n), I32),
        compiler_params=_cparams(("parallel",)),
        name="dest_rows",
    )(top_idx, rank, group_start.reshape(N_EXPERTS, 1))


def _row_copy(src, s, dst, d, sem):
    return pltpu.make_async_copy(src.at[pl.ds(s, 1), :], dst.at[pl.ds(d, 1), :], sem)


def _dispatch_kernel(dest_ref, pad_lo_ref, pad_n_ref, na_ref, x_ref, o_ref, zero_ref, sem, zsem, *, tm):
    i = pl.program_id(0)
    tq = x_ref.shape[0]
    n = pl.num_programs(0) * tq
    n_tiles = o_ref.shape[0] // tm

    @pl.when(i == 0)
    def _():
        zero_ref[...] = jnp.zeros_like(zero_ref)

        def pad_rows(e, total):
            def one(r, carry):
                _row_copy(zero_ref, 0, o_ref, pad_lo_ref[e] + r, zsem).start()
                return carry
            lax.fori_loop(0, pad_n_ref[e], one, 0)
            return total + pad_n_ref[e]

        n_pad = lax.fori_loop(0, N_EXPERTS, pad_rows, 0)

        def tile_copy(t):
            return pltpu.make_async_copy(zero_ref, o_ref.at[pl.ds(t * tm, tm), :], zsem)

        def start_tile(t, carry):
            tile_copy(t).start()
            return carry

        def drain_tile(t, carry):
            tile_copy(0).wait()
            return carry

        def drain_row(r, carry):
            _row_copy(zero_ref, 0, o_ref, 0, zsem).wait()
            return carry

        lax.fori_loop(na_ref[0], n_tiles, start_tile, 0)
        lax.fori_loop(0, n_pad, drain_row, 0)
        lax.fori_loop(na_ref[0], n_tiles, drain_tile, 0)

    def issue(r, carry):
        for k in range(TOP_K):
            _row_copy(x_ref, r, o_ref, dest_ref[k * n + i * tq + r], sem).start()
        return carry

    lax.fori_loop(0, tq, issue, 0, unroll=8)
    for _ in range(TOP_K):
        pltpu.make_async_copy(x_ref, o_ref.at[pl.ds(0, tq), :], sem).wait()


def _dispatch(dest_flat, pad_lo, pad_n, n_active, xn2, n_rows, tm):
    n, d = xn2.shape
    tq = min(256, n)
    return pl.pallas_call(
        functools.partial(_dispatch_kernel, tm=tm),
        grid_spec=pltpu.PrefetchScalarGridSpec(
            num_scalar_prefetch=4,
            grid=(n // tq,),
            in_specs=[pl.BlockSpec((tq, d), lambda i, *_: (i, 0))],
            out_specs=pl.BlockSpec(memory_space=pl.ANY),
            scratch_shapes=[pltpu.VMEM((tm, d), F32), pltpu.SemaphoreType.DMA(()), pltpu.SemaphoreType.DMA(())],
        ),
        out_shape=jax.ShapeDtypeStruct((n_rows, d), F32),
        compiler_params=_cparams(("arbitrary",)),
        name="dispatch_rows",
    )(dest_flat, pad_lo, pad_n, n_active, xn2)


def _gate_up_kernel(te_ref, na_ref, x_ref, w_ref, b_ref, o_ref):
    del te_ref
    active = pl.program_id(0) < na_ref[0]

    @pl.when(active)
    def _():
        x = x_ref[...].astype(BF16)
        cw = 512
        pick_even = (lax.broadcasted_iota(I32, (cw, cw // 2), 0)
                     == 2 * lax.broadcasted_iota(I32, (cw, cw // 2), 1)).astype(BF16)
        for c in range(2 * D_FF // cw):
            sl = slice(c * cw, (c + 1) * cw)
            hg = jnp.dot(x, w_ref[:, sl], preferred_element_type=F32) + b_ref[:, sl]
            nxt = jnp.concatenate(
                [pltpu.roll(hg[:, j * LANES:(j + 1) * LANES], LANES - 1, 1) for j in range(cw // LANES)], axis=1)
            gate = jnp.minimum(hg, SWIGLU_LIMIT)
            lin = jnp.clip(nxt, -SWIGLU_LIMIT, SWIGLU_LIMIT)
            act = (gate * _sigmoid(SWIGLU_ALPHA * gate) * (lin + 1.0)).astype(BF16)
            o_ref[:, c * cw // 2:(c + 1) * cw // 2] = jnp.dot(
                act, pick_even, preferred_element_type=F32).astype(o_ref.dtype)

    @pl.when(jnp.logical_not(active))
    def _():
        o_ref[...] = jnp.zeros_like(o_ref)


def _down_kernel(te_ref, na_ref, a_ref, wd_ref, bd_ref, o_ref):
    del te_ref
    active = pl.program_id(0) < na_ref[0]

    @pl.when(active)
    def _():
        o_ref[...] = jnp.dot(a_ref[...], wd_ref[...], preferred_element_type=F32) + bd_ref[...]

    @pl.when(jnp.logical_not(active))
    def _():
        o_ref[...] = jnp.zeros_like(o_ref)


def _expert_ffn(tile_expert, n_active, xs, wgu, bgu, wd, bd, tm):
    n_rows, d = xs.shape
    n_tiles = n_rows // tm
    f = wd.shape[1]
    row_in = lambda w: pl.BlockSpec((tm, w), lambda i, te, na: (jnp.minimum(i, na[0] - 1), 0))
    row_out = lambda w: pl.BlockSpec((tm, w), lambda i, te, na: (i, 0))
    per_expert = lambda a, b: pl.BlockSpec((None, a, b), lambda i, te, na: (te[i], 0, 0))
    act = pl.pallas_call(
        _gate_up_kernel,
        grid_spec=pltpu.PrefetchScalarGridSpec(
            num_scalar_prefetch=2,
            grid=(n_tiles,),
            in_specs=[row_in(d), per_expert(d, 2 * f), per_expert(1, 2 * f)],
            out_specs=row_out(f),
        ),
        out_shape=jax.ShapeDtypeStruct((n_rows, f), BF16),
        compiler_params=_cparams(("arbitrary",)),
        name="expert_gate_up",
    )(tile_expert, n_active, xs, wgu, bgu)
    return pl.pallas_call(
        _down_kernel,
        grid_spec=pltpu.PrefetchScalarGridSpec(
            num_scalar_prefetch=2,
            grid=(n_tiles,),
            in_specs=[row_in(f), per_expert(f, d), per_expert(1, d)],
            out_specs=row_out(d),
        ),
        out_shape=jax.ShapeDtypeStruct((n_rows, d), F32),
        compiler_params=_cparams(("arbitrary",)),
        name="expert_down",
    )(tile_expert, n_active, act, wd, bd)


def _combine_kernel(dest_ref, y_ref, tw_ref, h1_ref, fw_ref, o_ref, buf_ref, sem, *, final_norm):
    i = pl.program_id(0)
    steps = pl.num_programs(0)
    tq = h1_ref.shape[0]
    n = steps * tq

    def issue(step, slot):
        def body(r, carry):
            for k in range(TOP_K):
                src = dest_ref[k * n + step * tq + r]
                pltpu.make_async_copy(y_ref.at[pl.ds(src, 1), :], buf_ref.at[slot, k, pl.ds(r, 1), :],
                                      sem.at[slot]).start()
            return carry
        lax.fori_loop(0, tq, body, 0, unroll=8)

    @pl.when(i == 0)
    def _():
        issue(0, 0)

    @pl.when(i + 1 < steps)
    def _():
        issue(i + 1, (i + 1) % 2)

    slot = i % 2
    for k in range(TOP_K):
        pltpu.make_async_copy(y_ref.at[pl.ds(0, tq), :], buf_ref.at[slot, k], sem.at[slot]).wait()

    acc = h1_ref[...]
    moe = tw_ref[:, 0:1] * buf_ref[slot, 0]
    for k in range(1, TOP_K):
        moe = moe + tw_ref[:, k:k + 1] * buf_ref[slot, k]
    acc = acc + moe
    if final_norm:
        acc = acc * lax.rsqrt(jnp.mean(acc * acc, axis=-1, keepdims=True) + RMS_EPS) * fw_ref[...]
    o_ref[...] = acc


def _combine(dest_flat, y, top_w_t, h1, final_w, final_norm):
    n, d = h1.shape
    tq = min(256, n)
    return pl.pallas_call(
        functools.partial(_combine_kernel, final_norm=final_norm),
        grid_spec=pltpu.PrefetchScalarGridSpec(
            num_scalar_prefetch=1,
            grid=(n // tq,),
            in_specs=[pl.BlockSpec(memory_space=pl.ANY),
                      pl.BlockSpec((tq, TOP_K), lambda i, dest: (i, 0)),
                      pl.BlockSpec((tq, d), lambda i, dest: (i, 0)),
                      pl.BlockSpec((1, d), lambda i, dest: (0, 0))],
            out_specs=pl.BlockSpec((tq, d), lambda i, dest: (i, 0)),
            scratch_shapes=[pltpu.VMEM((2, TOP_K, tq, d), F32), pltpu.SemaphoreType.DMA((2,))],
        ),
        out_shape=jax.ShapeDtypeStruct((n, d), F32),
        compiler_params=_cparams(("arbitrary",)),
        name="combine_rows",
    )(dest_flat, y, top_w_t, h1, final_w.reshape(1, d))


def _reorder_in_columns(w):
    kv0 = 2 * RET_QK_W + 2 * RET_V_W + SWA_Q_W
    kv1 = kv0 + 2 * SWA_KV_W
    return jnp.concatenate([w[:, :kv0], w[:, kv1:], w[:, kv0:kv1]], axis=1)


def _layer(h, tables, p, final_w, final_norm, expert_tile):
    b, s, d = h.shape
    n = b * s
    cr, sr, cs, sa, sb = tables
    x2 = h.reshape(n, d)
    proj = _in_projection(x2, p["attn_norm_w"], _reorder_in_columns(p["w_in"]).astype(BF16))
    proj3 = proj.reshape(b, s, IN_WIDTH)
    t3 = lambda t: t.reshape(b, s, LANES)
    ret = _retention(proj3, t3(cr), t3(sr))
    swa = _sliding_window(proj3, p["sinks"], t3(cs), t3(sa), t3(sb))
    h1, xn2, top_idx, top_w, rank, counts = _post_attention(
        ret.reshape(n, RET_V_W), swa.reshape(n, SWA_Q_W), proj, x2, p["gate_bias"],
        p["w_ret_out"].astype(BF16), p["w_swa_out"].astype(BF16), p["w_o"].astype(BF16),
        p["ffn_norm_w"], p["router_w"], p["router_b"])

    tm = expert_tile
    n_tiles = (n * TOP_K) // tm + N_EXPERTS
    cnt = counts[:, 0].astype(I32)
    padded = ((cnt + tm - 1) // tm) * tm
    ends = jnp.cumsum(padded)
    group_start = ends - padded
    n_active = (ends[-1] // tm).reshape(1)
    tile_ids = jnp.arange(n_tiles, dtype=I32)
    tile_expert = jnp.minimum(
        jnp.sum((tile_ids[:, None] >= (ends // tm)[None, :]).astype(I32), axis=1), N_EXPERTS - 1)

    dest_flat = _dest_rows(top_idx, rank, group_start).reshape(TOP_K * n)
    xs = _dispatch(dest_flat, group_start + cnt, padded - cnt, n_active, xn2, n_tiles * tm, tm)

    y = _expert_ffn(tile_expert, n_active, xs, p["w_gate_up"].astype(BF16),
                    p["b_gate_up"].reshape(N_EXPERTS, 1, 2 * D_FF), p["w_down"].astype(BF16),
                    p["b_down"].reshape(N_EXPERTS, 1, d), tm)
    out = _combine(dest_flat, y, top_w.T, h1, final_w, final_norm)
    return out.reshape(b, s, d)


def kernel(x, positions, attn_norm_w, w_in, gate_bias, w_ret_out, w_swa_out, w_o, sinks, ffn_norm_w, router_w,
           router_b, w_gate_up, b_gate_up, w_down, b_down, final_norm_w):
    depth = w_in.shape[0]
    stacked = dict(attn_norm_w=attn_norm_w, w_in=w_in, gate_bias=gate_bias, w_ret_out=w_ret_out,
                   w_swa_out=w_swa_out, w_o=w_o, sinks=sinks, ffn_norm_w=ffn_norm_w, router_w=router_w,
                   router_b=router_b, w_gate_up=w_gate_up, b_gate_up=b_gate_up, w_down=w_down, b_down=b_down)
    tables = _rope_tables(positions)
    h = x
    for layer in range(depth):
        p = {name: w[layer] for name, w in stacked.items()}
        h = _layer(h, tables, p, final_norm_w, layer == depth - 1, expert_tile=256)
    return h
```

```python
import functools
import math

import jax
import jax.numpy as jnp
from jax import lax
from jax.experimental import pallas as pl
from jax.experimental.pallas import tpu as pltpu

F32 = jnp.float32
BF16 = jnp.bfloat16
I32 = jnp.int32

D_MODEL = 2048
RMS_EPS = 1e-5
ROPE_THETA = 10000.0
RET_HEADS = 8
RET_DK = 128
RET_DV = 256
RET_CHUNK = 128
SWA_Q_HEADS = 32
SWA_KV_HEADS = 4
SWA_HEAD_DIM = 64
SWA_WINDOW = 128
SWA_BLOCK = 128
N_EXPERTS = 32
TOP_K = 4
D_FF = 2048
SWIGLU_LIMIT = 7.0
SWIGLU_ALPHA = 1.702

RET_QK_W = RET_HEADS * RET_DK
RET_V_W = RET_HEADS * RET_DV
SWA_Q_W = SWA_Q_HEADS * SWA_HEAD_DIM
SWA_KV_W = SWA_KV_HEADS * SWA_HEAD_DIM
GATE_W = 2 * D_MODEL
IN_WIDTH = 2 * RET_QK_W + 2 * RET_V_W + SWA_Q_W + 2 * SWA_KV_W + GATE_W

LANES = 128
VMEM_LIMIT = 56 * 1024 * 1024
NEG_BIG = -1e30

_NT = (((1,), (1,)), ((), ()))
_TN = (((0,), (0,)), ((), ()))


def _cparams(sem):
    return pltpu.CompilerParams(dimension_semantics=sem, vmem_limit_bytes=VMEM_LIMIT)


def _sigmoid(x):
    return 1.0 / (1.0 + jnp.exp(-x))


def _rope_table_kernel(pos_ref, invr_ref, invs_ref, cr_ref, sr_ref, cs_ref, sa_ref, sb_ref):
    pos = pos_ref[...]
    lane = lax.broadcasted_iota(I32, (pos.shape[0], LANES), 1)
    ang_r = pos * invr_ref[...]
    cr_ref[...] = jnp.cos(ang_r)
    sr_ref[...] = jnp.where(lane < RET_DK // 2, -1.0, 1.0) * jnp.sin(ang_r)
    ang_s = pos * invs_ref[...]
    sin_s = jnp.sin(ang_s)
    first_half = (lane % SWA_HEAD_DIM) < SWA_HEAD_DIM // 2
    cs_ref[...] = jnp.cos(ang_s)
    sa_ref[...] = jnp.where(first_half, -sin_s, 0.0)
    sb_ref[...] = jnp.where(first_half, 0.0, sin_s)


def _rope_tables(positions):
    n = positions.size
    pos = positions.reshape(n, 1).astype(F32)
    inv_r = 1.0 / (ROPE_THETA ** (jnp.arange(0, RET_DK, 2, dtype=F32) / RET_DK))
    inv_s = 1.0 / (ROPE_THETA ** (jnp.arange(0, SWA_HEAD_DIM, 2, dtype=F32) / SWA_HEAD_DIM))
    inv_r = jnp.tile(inv_r, LANES // inv_r.size).reshape(1, LANES)
    inv_s = jnp.tile(inv_s, LANES // inv_s.size).reshape(1, LANES)
    t = min(2048, n)
    tab = pl.BlockSpec((t, LANES), lambda i: (i, 0))
    row = pl.BlockSpec((1, LANES), lambda i: (0, 0))
    return pl.pallas_call(
        _rope_table_kernel,
        grid=(n // t,),
        in_specs=[pl.BlockSpec((t, 1), lambda i: (i, 0)), row, row],
        out_specs=[tab] * 5,
        out_shape=[jax.ShapeDtypeStruct((n, LANES), F32)] * 5,
        compiler_params=_cparams(("parallel",)),
        name="rope_tables",
    )(pos, inv_r, inv_s)


def _inproj_kernel(x_ref, nw_ref, w_ref, o_ref, xn_ref):
    @pl.when(pl.program_id(1) == 0)
    def _():
        x = x_ref[...]
        ms = jnp.mean(x * x, axis=-1, keepdims=True)
        xn_ref[...] = (x * lax.rsqrt(ms + RMS_EPS) * nw_ref[...]).astype(BF16)

    o_ref[...] = jnp.dot(xn_ref[...], w_ref[...], preferred_element_type=F32).astype(o_ref.dtype)


def _in_projection(x2, norm_w, w_in_bf16):
    n, d = x2.shape
    width = w_in_bf16.shape[1]
    tm = min(1024, n)
    tn = 1280
    return pl.pallas_call(
        _inproj_kernel,
        grid=(n // tm, width // tn),
        in_specs=[pl.BlockSpec((tm, d), lambda i, j: (i, 0)),
                  pl.BlockSpec((1, d), lambda i, j: (0, 0)),
                  pl.BlockSpec((d, tn), lambda i, j: (0, j))],
        out_specs=pl.BlockSpec((tm, tn), lambda i, j: (i, j)),
        out_shape=jax.ShapeDtypeStruct((n, width), BF16),
        scratch_shapes=[pltpu.VMEM((tm, d), BF16)],
        compiler_params=_cparams(("parallel", "arbitrary")),
        name="in_projection",
    )(x2, norm_w.reshape(1, d), w_in_bf16)


def _retention_kernel(q_ref, k_ref, v_ref, g_ref, cr_ref, sr_ref, o_ref, st_ref):
    @pl.when(pl.program_id(1) == 0)
    def _():
        st_ref[...] = jnp.zeros_like(st_ref)

    c = RET_CHUNK
    cr = cr_ref[...]
    sr = sr_ref[...]
    row = lax.broadcasted_iota(I32, (c, c), 0)
    col = lax.broadcasted_iota(I32, (c, c), 1)
    rel = (row - col).astype(F32)
    n_row = row.astype(F32)
    scale = RET_DK ** -0.5
    for h in range(RET_HEADS):
        log_gamma = math.log1p(-(2.0 ** (-5.0 - h)))
        q = q_ref[:, h * RET_DK:(h + 1) * RET_DK].astype(F32)
        k = k_ref[:, h * RET_DK:(h + 1) * RET_DK].astype(F32)
        qr = q * cr + pltpu.roll(q, RET_DK // 2, 1) * sr
        kr = k * cr + pltpu.roll(k, RET_DK // 2, 1) * sr
        decay = jnp.where(rel >= 0, jnp.exp(log_gamma * jnp.maximum(rel, 0.0)), 0.0) * scale
        s = lax.dot_general(qr.astype(BF16), kr.astype(BF16), _NT, preferred_element_type=F32) * decay
        v = v_ref[:, h * RET_DV:(h + 1) * RET_DV]
        inner = jnp.dot(s.astype(BF16), v, preferred_element_type=F32)
        xi = jnp.exp(log_gamma * (n_row + 1.0))
        st = st_ref[h]
        cross = jnp.dot((qr * xi).astype(BF16), st.astype(BF16), preferred_element_type=F32)
        o = inner + cross
        zeta = jnp.exp(log_gamma * (c - 1.0 - n_row)) * scale
        u = lax.dot_general((kr * zeta).astype(BF16), v, _TN, preferred_element_type=F32)
        st_ref[h] = math.exp(log_gamma * c) * st + u
        o = o * lax.rsqrt(jnp.mean(o * o, axis=-1, keepdims=True) + RMS_EPS)
        g = g_ref[:, h * RET_DV:(h + 1) * RET_DV].astype(F32)
        o_ref[:, h * RET_DV:(h + 1) * RET_DV] = (g * _sigmoid(g) * o).astype(o_ref.dtype)


def _retention(proj3, cr3, sr3):
    b, s, _ = proj3.shape
    c = RET_CHUNK
    return pl.pallas_call(
        _retention_kernel,
        grid=(b, s // c),
        in_specs=[pl.BlockSpec((None, c, RET_QK_W), lambda i, j: (i, j, 0)),
                  pl.BlockSpec((None, c, RET_QK_W), lambda i, j: (i, j, 1)),
                  pl.BlockSpec((None, c, RET_V_W), lambda i, j: (i, j, 1)),
                  pl.BlockSpec((None, c, RET_V_W), lambda i, j: (i, j, 2)),
                  pl.BlockSpec((None, c, LANES), lambda i, j: (i, j, 0)),
                  pl.BlockSpec((None, c, LANES), lambda i, j: (i, j, 0))],
        out_specs=pl.BlockSpec((None, c, RET_V_W), lambda i, j: (i, j, 0)),
        out_shape=jax.ShapeDtypeStruct((b, s, RET_V_W), BF16),
        scratch_shapes=[pltpu.VMEM((RET_HEADS, RET_DK, RET_DV), F32)],
        compiler_params=_cparams(("arbitrary", "arbitrary")),
        name="retention",
    )(proj3, proj3, proj3, proj3, cr3, sr3)


def _swa_kernel(sinks_ref, q_ref, kv_ref, kvp_ref, cs_ref, sa_ref, sb_ref, csp_ref, sap_ref, sbp_ref, o_ref):
    blk = pl.program_id(1)
    wb = SWA_BLOCK
    hd = SWA_HEAD_DIM
    group = SWA_Q_HEADS // SWA_KV_HEADS

    def rope(x, cs, sa, sb):
        return x * cs + pltpu.roll(x, LANES - hd // 2, 1) * sa + pltpu.roll(x, hd // 2, 1) * sb

    def rope_keys(ref, tabs):
        return [rope(ref[:, j * LANES:(j + 1) * LANES].astype(F32), *tabs) for j in range(SWA_KV_W // LANES)]

    cur_tabs = (cs_ref[...], sa_ref[...], sb_ref[...])
    kc = rope_keys(kv_ref, cur_tabs)
    kp = rope_keys(kvp_ref, (csp_ref[...], sap_ref[...], sbp_ref[...]))
    vc = kv_ref[:, SWA_KV_W:2 * SWA_KV_W].astype(F32)
    vp = kvp_ref[:, SWA_KV_W:2 * SWA_KV_W].astype(F32)

    qr = [rope(q_ref[:, j * LANES:(j + 1) * LANES].astype(F32), *cur_tabs) * (hd ** -0.5)
          for j in range(SWA_Q_W // LANES)]

    def head_cols(slabs, h):
        return slabs[h // 2][:, (h % 2) * hd:(h % 2 + 1) * hd]

    cols = group * wb
    kj = lax.broadcasted_iota(I32, (2 * wb, cols), 0)
    qi = lax.broadcasted_iota(I32, (2 * wb, cols), 1) & (wb - 1)
    valid = (kj > qi) & (kj <= qi + SWA_WINDOW) & ((kj >= wb) | (blk > 0))

    for kh in range(SWA_KV_HEADS):
        kband = jnp.concatenate([head_cols(kp, kh), head_cols(kc, kh)], axis=0).astype(BF16)
        vband = jnp.concatenate([vp[:, kh * hd:(kh + 1) * hd], vc[:, kh * hd:(kh + 1) * hd]], axis=0).astype(BF16)
        heads = [kh * group + g for g in range(group)]
        qg = jnp.concatenate([head_cols(qr, h) for h in heads], axis=0).astype(BF16)
        sink = jnp.concatenate([jnp.full((1, wb), sinks_ref[h], F32) for h in heads], axis=1)
        s = lax.dot_general(kband, qg, _NT, preferred_element_type=F32)
        s = jnp.where(valid, s, NEG_BIG)
        m = jnp.maximum(jnp.max(s, axis=0, keepdims=True), sink)
        p = jnp.exp(s - m)
        denom = jnp.sum(p, axis=0, keepdims=True) + jnp.exp(sink - m)
        o = lax.dot_general((p * (1.0 / denom)).astype(BF16), vband, _TN, preferred_element_type=F32)
        for g in range(0, group, 2):
            pair = jnp.concatenate([o[g * wb:(g + 1) * wb], o[(g + 1) * wb:(g + 2) * wb]], axis=1)
            h = heads[g]
            o_ref[:, h * hd:(h + 2) * hd] = pair.astype(o_ref.dtype)


def _sliding_window(proj3, sinks, cs3, sa3, sb3):
    b, s, _ = proj3.shape
    wb = SWA_BLOCK
    q_blk = (2 * RET_QK_W + 2 * RET_V_W) // SWA_Q_W
    kv_blk = (2 * RET_QK_W + 2 * RET_V_W + SWA_Q_W + GATE_W) // (2 * SWA_KV_W)
    prev = lambda j: jnp.maximum(j - 1, 0)
    tab = pl.BlockSpec((None, wb, LANES), lambda i, j: (i, j, 0))
    tab_prev = pl.BlockSpec((None, wb, LANES), lambda i, j: (i, prev(j), 0))
    return pl.pallas_call(
        _swa_kernel,
        grid=(b, s // wb),
        in_specs=[pl.BlockSpec(memory_space=pltpu.SMEM),
                  pl.BlockSpec((None, wb, SWA_Q_W), lambda i, j: (i, j, q_blk)),
                  pl.BlockSpec((None, wb, 2 * SWA_KV_W), lambda i, j: (i, j, kv_blk)),
                  pl.BlockSpec((None, wb, 2 * SWA_KV_W), lambda i, j: (i, prev(j), kv_blk)),
                  tab, tab, tab, tab_prev, tab_prev, tab_prev],
        out_specs=pl.BlockSpec((None, wb, SWA_Q_W), lambda i, j: (i, j, 0)),
        out_shape=jax.ShapeDtypeStruct((b, s, SWA_Q_W), BF16),
        compiler_params=_cparams(("parallel", "parallel")),
        name="sliding_window",
    )(sinks, proj3, proj3, proj3, cs3, sa3, sb3, cs3, sa3, sb3)


def _post_kernel(ret_ref, swa_ref, gl_ref, x_ref, gb_ref, wro_ref, wso_ref, wo_ref, fw_ref, rwt_ref, rb_ref,
                 h1_ref, xn2_ref, idx_ref, tw_ref, rank_ref, cnt_ref, carry_ref):
    @pl.when(pl.program_id(0) == 0)
    def _():
        carry_ref[...] = jnp.zeros_like(carry_ref)

    d = D_MODEL
    a = jnp.dot(ret_ref[...], wro_ref[...], preferred_element_type=F32)
    b = jnp.dot(swa_ref[...], wso_ref[...], preferred_element_type=F32)
    gl = gl_ref[...].astype(F32) + gb_ref[...]
    mix = _sigmoid(gl[:, :d]) * a + _sigmoid(gl[:, d:]) * b
    h1 = x_ref[...] + jnp.dot(mix.astype(BF16), wo_ref[...], preferred_element_type=F32)
    h1_ref[...] = h1
    xn2 = h1 * lax.rsqrt(jnp.mean(h1 * h1, axis=-1, keepdims=True) + RMS_EPS) * fw_ref[...]
    xn2_ref[...] = xn2

    xh = xn2.astype(BF16)
    xl = (xn2 - xh.astype(F32)).astype(BF16)
    rw = rwt_ref[...]
    rh = rw.astype(BF16)
    rl = (rw - rh.astype(F32)).astype(BF16)
    logits = (lax.dot_general(rh, xh, _NT, preferred_element_type=F32)
              + lax.dot_general(rh, xl, _NT, preferred_element_type=F32)
              + lax.dot_general(rl, xh, _NT, preferred_element_type=F32)) + rb_ref[...]
    tm = logits.shape[1]
    e_iota = lax.broadcasted_iota(I32, (N_EXPERTS, tm), 0)
    vals, idxs = [], []
    for _ in range(TOP_K):
        m = jnp.max(logits, axis=0, keepdims=True)
        ix = jnp.min(jnp.where(logits == m, e_iota, N_EXPERTS), axis=0, keepdims=True)
        vals.append(m)
        idxs.append(ix)
        logits = jnp.where(e_iota == ix, -jnp.inf, logits)
    ex = [jnp.exp(v - vals[0]) for v in vals]
    den = ex[0] + ex[1] + ex[2] + ex[3]
    tw_ref[...] = jnp.concatenate([e / den for e in ex], axis=0)
    idx_ref[...] = jnp.concatenate(idxs, axis=0)

    onehot = jnp.zeros((N_EXPERTS, tm), F32)
    for ix in idxs:
        onehot = onehot + (e_iota == ix).astype(F32)
    earlier = (lax.broadcasted_iota(I32, (tm, tm), 0) < lax.broadcasted_iota(I32, (tm, tm), 1)).astype(BF16)
    prefix = jnp.dot(onehot.astype(BF16), earlier, preferred_element_type=F32) + carry_ref[:, 0:1]
    ranks = [jnp.sum(jnp.where(e_iota == ix, prefix, 0.0), axis=0, keepdims=True) for ix in idxs]
    rank_ref[...] = jnp.concatenate(ranks, axis=0).astype(I32)
    carry = carry_ref[...] + jnp.sum(onehot, axis=1, keepdims=True)
    carry_ref[...] = carry
    cnt_ref[...] = carry


def _post_attention(ret2, swa2, proj2, x2, gate_bias, wro, wso, wo, ffn_w, router_w, router_b):
    n, d = x2.shape
    tm = min(256, n)
    gl_blk = (2 * RET_QK_W + 2 * RET_V_W + SWA_Q_W) // GATE_W
    const = lambda shape: pl.BlockSpec(shape, lambda i: (0,) * len(shape), pipeline_mode=pl.Buffered(1))
    rowblk = lambda w: pl.BlockSpec((tm, w), lambda i: (i, 0))
    tokrow = pl.BlockSpec((TOP_K, tm), lambda i: (0, i))
    return pl.pallas_call(
        _post_kernel,
        grid=(n // tm,),
        in_specs=[rowblk(RET_V_W), rowblk(SWA_Q_W),
                  pl.BlockSpec((tm, GATE_W), lambda i: (i, gl_blk)),
                  rowblk(d), const((1, GATE_W)),
                  const((RET_V_W, d)), const((SWA_Q_W, d)), const((d, d)),
                  const((1, d)), const((N_EXPERTS, d)), const((N_EXPERTS, 1))],
        out_specs=[rowblk(d), rowblk(d), tokrow, tokrow, tokrow,
                   pl.BlockSpec((N_EXPERTS, LANES), lambda i: (0, 0))],
        out_shape=[jax.ShapeDtypeStruct((n, d), F32), jax.ShapeDtypeStruct((n, d), F32),
                   jax.ShapeDtypeStruct((TOP_K, n), I32), jax.ShapeDtypeStruct((TOP_K, n), F32),
                   jax.ShapeDtypeStruct((TOP_K, n), I32), jax.ShapeDtypeStruct((N_EXPERTS, LANES), F32)],
        scratch_shapes=[pltpu.VMEM((N_EXPERTS, LANES), F32)],
        compiler_params=_cparams(("arbitrary",)),
        name="post_attention",
    )(ret2, swa2, proj2, x2, gate_bias.reshape(1, GATE_W), wro, wso, wo, ffn_w.reshape(1, d),
      router_w.T, router_b.reshape(N_EXPERTS, 1))


def _dest_kernel(idx_ref, rank_ref, gs_ref, o_ref):
    t = idx_ref.shape[1]
    e_iota = lax.broadcasted_iota(I32, (N_EXPERTS, t), 0)
    gs = gs_ref[...]
    rows = []
    for k in range(TOP_K):
        start = jnp.sum(jnp.where(e_iota == idx_ref[k:k + 1, :], gs, 0), axis=0, keepdims=True)
        rows.append(start + rank_ref[k:k + 1, :])
    o_ref[...] = jnp.concatenate(rows, axis=0)


def _dest_rows(top_idx, rank, group_start):
    n = top_idx.shape[1]
    t = min(2048, n)
    blk = pl.BlockSpec((TOP_K, t), lambda i: (0, i))
    return pl.pallas_call(
        _dest_kernel,
        grid=(n // t,),
        in_specs=[blk, blk, pl.BlockSpec((N_EXPERTS, 1), lambda i: (0, 0))],
        out_specs=blk,
        out_shape=jax.ShapeDtypeStruct((TOP_K, n), I32),
        compiler_params=_cparams(("parallel",)),
        name="dest_rows",
    )(top_idx, rank, group_start.reshape(N_EXPERTS, 1))


def _row_copy(src, s, dst, d, sem):
    return pltpu.make_async_copy(src.at[pl.ds(s, 1), :], dst.at[pl.ds(d, 1), :], sem)


def _dispatch_kernel(dest_ref, pad_lo_ref, pad_n_ref, na_ref, x_ref, o_ref, zero_ref, sem, zsem, *, tm):
    i = pl.program_id(0)
    tq = x_ref.shape[0]
    n = pl.num_programs(0) * tq
    n_tiles = o_ref.shape[0] // tm

    @pl.when(i == 0)
    def _():
        zero_ref[...] = jnp.zeros_like(zero_ref)

        def pad_rows(e, total):
            def one(r, carry):
                _row_copy(zero_ref, 0, o_ref, pad_lo_ref[e] + r, zsem).start()
                return carry
            lax.fori_loop(0, pad_n_ref[e], one, 0)
            return total + pad_n_ref[e]

        n_pad = lax.fori_loop(0, N_EXPERTS, pad_rows, 0)

        def tile_copy(t):
            return pltpu.make_async_copy(zero_ref, o_ref.at[pl.ds(t * tm, tm), :], zsem)

        def start_tile(t, carry):
            tile_copy(t).start()
            return carry

        def drain_tile(t, carry):
            tile_copy(0).wait()
            return carry

        def drain_row(r, carry):
            _row_copy(zero_ref, 0, o_ref, 0, zsem).wait()
            return carry

        lax.fori_loop(na_ref[0], n_tiles, start_tile, 0)
        lax.fori_loop(0, n_pad, drain_row, 0)
        lax.fori_loop(na_ref[0], n_tiles, drain_tile, 0)

    def issue(r, carry):
        for k in range(TOP_K):
            _row_copy(x_ref, r, o_ref, dest_ref[k * n + i * tq + r], sem).start()
        return carry

    lax.fori_loop(0, tq, issue, 0, unroll=8)
    for _ in range(TOP_K):
        pltpu.make_async_copy(x_ref, o_ref.at[pl.ds(0, tq), :], sem).wait()


def _dispatch(dest_flat, pad_lo, pad_n, n_active, xn2, n_rows, tm):
    n, d = xn2.shape
    tq = min(256, n)
    return pl.pallas_call(
        functools.partial(_dispatch_kernel, tm=tm),
        grid_spec=pltpu.PrefetchScalarGridSpec(
            num_scalar_prefetch=4,
            grid=(n // tq,),
            in_specs=[pl.BlockSpec((tq, d), lambda i, *_: (i, 0))],
            out_specs=pl.BlockSpec(memory_space=pl.ANY),
            scratch_shapes=[pltpu.VMEM((tm, d), F32), pltpu.SemaphoreType.DMA(()), pltpu.SemaphoreType.DMA(())],
        ),
        out_shape=jax.ShapeDtypeStruct((n_rows, d), F32),
        compiler_params=_cparams(("arbitrary",)),
        name="dispatch_rows",
    )(dest_flat, pad_lo, pad_n, n_active, xn2)


GU_CHUNK = 512
DN_CHUNK = 512


IDLE_STEP = -1
FILL_STEP = -2


def _work_schedule(tiles_per_expert, n_chunks, n_steps, n_tiles):
    e_ids = jnp.arange(N_EXPERTS, dtype=I32)
    steps_e = jnp.where(e_ids < N_EXPERTS - 1, jnp.maximum(tiles_per_expert, n_chunks), tiles_per_expert)
    end_e = n_chunks + jnp.cumsum(steps_e)
    start_e = end_e - steps_e
    first_tile_e = jnp.cumsum(tiles_per_expert) - tiles_per_expert
    w = jnp.arange(n_steps, dtype=I32)
    prologue = w < n_chunks
    owner = jnp.sum((w[:, None] >= end_e[None, :]).astype(I32), axis=1)
    live = jnp.logical_and(jnp.logical_not(prologue), owner < N_EXPERTS)
    own = jnp.minimum(owner, N_EXPERTS - 1)
    local = w - start_e[own]
    has_tile = jnp.logical_and(live, local < tiles_per_expert[own])
    has_conv = jnp.logical_or(prologue, jnp.logical_and(jnp.logical_and(live, own < N_EXPERTS - 1),
                                                        local < n_chunks))
    n_used = jnp.sum(tiles_per_expert)
    spare = w - end_e[N_EXPERTS - 1]
    fill = jnp.logical_and(spare >= 0, n_used + spare < n_tiles)
    in_blk = jnp.maximum(jnp.cumsum(has_tile.astype(I32)) - 1, 0)
    out_blk = jnp.maximum(jnp.cumsum(jnp.logical_or(has_tile, fill).astype(I32)) - 1, 0)
    conv_lin = jnp.maximum(jnp.cumsum(has_conv.astype(I32)) - 1, 0)
    tile = jnp.where(has_tile, first_tile_e[own] + local, jnp.where(fill, FILL_STEP, IDLE_STEP))
    cur_expert = jnp.where(prologue, 0, own)
    return (tile.astype(I32), in_blk.astype(I32), out_blk.astype(I32), has_conv.astype(I32),
            (conv_lin // n_chunks).astype(I32), (conv_lin % n_chunks).astype(I32), (cur_expert % 2).astype(I32),
            cur_expert.astype(I32))


def _gate_up_kernel(tile_ref, iblk_ref, oblk_ref, conv_ref, cexp_ref, cchunk_ref, slot_ref, bexp_ref,
                    x_ref, w32_ref, bg_ref, bl_ref, o_ref, wbuf_ref):
    del iblk_ref, oblk_ref, bexp_ref
    step = pl.program_id(0)

    @pl.when(tile_ref[step] == FILL_STEP)
    def _():
        o_ref[...] = jnp.zeros_like(o_ref)

    half = GU_CHUNK // 2
    n_slabs = 2 * D_FF // GU_CHUNK

    @pl.when(conv_ref[step] == 1)
    def _():
        src = lax.broadcasted_iota(I32, (half, half), 0)
        dst = lax.broadcasted_iota(I32, (half, half), 1)
        perm = (src == jnp.where(dst < half // 2, 2 * dst, 2 * (dst - half // 2) + 1)).astype(BF16)
        parts = [jnp.dot(w32_ref[:, j * half:(j + 1) * half].astype(BF16), perm,
                         preferred_element_type=F32).astype(BF16) for j in range(2)]
        cslot = cexp_ref[step] % 2
        c = cchunk_ref[step]
        wbuf_ref[cslot, c] = jnp.concatenate([p[:, :half // 2] for p in parts], axis=1)
        wbuf_ref[cslot, n_slabs + c] = jnp.concatenate([p[:, half // 2:] for p in parts], axis=1)

    @pl.when(tile_ref[step] >= 0)
    def _():
        slot = slot_ref[step]
        x = x_ref[...].astype(BF16)
        per = 2
        for c in range(0, n_slabs, per):
            sl = slice(c * half, (c + per) * half)
            wg = jnp.concatenate([wbuf_ref[slot, c + j] for j in range(per)], axis=1)
            wl = jnp.concatenate([wbuf_ref[slot, n_slabs + c + j] for j in range(per)], axis=1)
            gate = jnp.dot(x, wg, preferred_element_type=F32) + bg_ref[:, sl]
            lin = jnp.dot(x, wl, preferred_element_type=F32) + bl_ref[:, sl]
            gate = jnp.minimum(gate, SWIGLU_LIMIT)
            lin = jnp.clip(lin, -SWIGLU_LIMIT, SWIGLU_LIMIT)
            o_ref[:, sl] = (gate * _sigmoid(SWIGLU_ALPHA * gate) * (lin + 1.0)).astype(o_ref.dtype)


def _down_kernel(tile_ref, iblk_ref, oblk_ref, conv_ref, cexp_ref, cchunk_ref, slot_ref, bexp_ref,
                 a_ref, w32_ref, bd_ref, o_ref, wbuf_ref):
    del iblk_ref, oblk_ref, bexp_ref
    step = pl.program_id(0)

    @pl.when(tile_ref[step] == FILL_STEP)
    def _():
        o_ref[...] = jnp.zeros_like(o_ref)


    @pl.when(conv_ref[step] == 1)
    def _():
        wbuf_ref[cexp_ref[step] % 2, cchunk_ref[step]] = w32_ref[...].astype(BF16)

    @pl.when(tile_ref[step] >= 0)
    def _():
        slot = slot_ref[step]
        acc = bd_ref[...] + jnp.dot(a_ref[:, :DN_CHUNK], wbuf_ref[slot, 0], preferred_element_type=F32)
        for c in range(1, D_FF // DN_CHUNK):
            acc = acc + jnp.dot(a_ref[:, c * DN_CHUNK:(c + 1) * DN_CHUNK], wbuf_ref[slot, c],
                                preferred_element_type=F32)
        o_ref[...] = acc


def _expert_ffn(tiles_per_expert, xs, wgu, bg, bl, wd, bd, tm):
    n_rows, d = xs.shape
    n_tiles = n_rows // tm
    f = wd.shape[1]
    rows_in = lambda w: pl.BlockSpec((tm, w), lambda s, tile, iblk, *_: (iblk[s], 0))
    rows_out = lambda w: pl.BlockSpec((tm, w), lambda s, tile, iblk, oblk, *_: (oblk[s], 0))
    bias = lambda w: pl.BlockSpec((None, 1, w), lambda s, t, i, o, c, ce, cc, sl, bexp: (bexp[s], 0, 0))

    gu_chunks = 2 * f // GU_CHUNK
    gu_steps = gu_chunks + n_tiles + gu_chunks * N_EXPERTS
    act = pl.pallas_call(
        _gate_up_kernel,
        grid_spec=pltpu.PrefetchScalarGridSpec(
            num_scalar_prefetch=8,
            grid=(gu_steps,),
            in_specs=[rows_in(d),
                      pl.BlockSpec((None, d, GU_CHUNK),
                                   lambda s, t, i, o, c, cexp, cchunk, *_: (cexp[s], 0, cchunk[s])),
                      bias(f), bias(f)],
            out_specs=rows_out(f),
            scratch_shapes=[pltpu.VMEM((2, 2 * gu_chunks, d, GU_CHUNK // 2), BF16)],
        ),
        out_shape=jax.ShapeDtypeStruct((n_rows, f), BF16),
        compiler_params=_cparams(("arbitrary",)),
        name="expert_gate_up",
    )(*_work_schedule(tiles_per_expert, gu_chunks, gu_steps, n_tiles), xs, wgu, bg, bl)

    dn_chunks = f // DN_CHUNK
    dn_steps = dn_chunks + n_tiles + dn_chunks * N_EXPERTS
    return pl.pallas_call(
        _down_kernel,
        grid_spec=pltpu.PrefetchScalarGridSpec(
            num_scalar_prefetch=8,
            grid=(dn_steps,),
            in_specs=[rows_in(f),
                      pl.BlockSpec((None, DN_CHUNK, d),
                                   lambda s, t, i, o, c, cexp, cchunk, *_: (cexp[s], cchunk[s], 0)),
                      bias(d)],
            out_specs=rows_out(d),
            scratch_shapes=[pltpu.VMEM((2, dn_chunks, DN_CHUNK, d), BF16)],
        ),
        out_shape=jax.ShapeDtypeStruct((n_rows, d), F32),
        compiler_params=_cparams(("arbitrary",)),
        name="expert_down",
    )(*_work_schedule(tiles_per_expert, dn_chunks, dn_steps, n_tiles), act, wd, bd)


def _combine_kernel(dest_ref, y_ref, tw_ref, h1_ref, fw_ref, o_ref, buf_ref, sem, *, final_norm):
    i = pl.program_id(0)
    steps = pl.num_programs(0)
    tq = h1_ref.shape[0]
    n = steps * tq

    def issue(step, slot):
        def body(r, carry):
            for k in range(TOP_K):
                src = dest_ref[k * n + step * tq + r]
                pltpu.make_async_copy(y_ref.at[pl.ds(src, 1), :], buf_ref.at[slot, k, pl.ds(r, 1), :],
                                      sem.at[slot]).start()
            return carry
        lax.fori_loop(0, tq, body, 0, unroll=8)

    @pl.when(i == 0)
    def _():
        issue(0, 0)

    @pl.when(i + 1 < steps)
    def _():
        issue(i + 1, (i + 1) % 2)

    slot = i % 2
    for k in range(TOP_K):
        pltpu.make_async_copy(y_ref.at[pl.ds(0, tq), :], buf_ref.at[slot, k], sem.at[slot]).wait()

    acc = h1_ref[...]
    moe = tw_ref[:, 0:1] * buf_ref[slot, 0]
    for k in range(1, TOP_K):
        moe = moe + tw_ref[:, k:k + 1] * buf_ref[slot, k]
    acc = acc + moe
    if final_norm:
        acc = acc * lax.rsqrt(jnp.mean(acc * acc, axis=-1, keepdims=True) + RMS_EPS) * fw_ref[...]
    o_ref[...] = acc


def _combine(dest_flat, y, top_w_t, h1, final_w, final_norm):
    n, d = h1.shape
    tq = min(256, n)
    return pl.pallas_call(
        functools.partial(_combine_kernel, final_norm=final_norm),
        grid_spec=pltpu.PrefetchScalarGridSpec(
            num_scalar_prefetch=1,
            grid=(n // tq,),
            in_specs=[pl.BlockSpec(memory_space=pl.ANY),
                      pl.BlockSpec((tq, TOP_K), lambda i, dest: (i, 0)),
                      pl.BlockSpec((tq, d), lambda i, dest: (i, 0)),
                      pl.BlockSpec((1, d), lambda i, dest: (0, 0))],
            out_specs=pl.BlockSpec((tq, d), lambda i, dest: (i, 0)),
            scratch_shapes=[pltpu.VMEM((2, TOP_K, tq, d), F32), pltpu.SemaphoreType.DMA((2,))],
        ),
        out_shape=jax.ShapeDtypeStruct((n, d), F32),
        compiler_params=_cparams(("arbitrary",)),
        name="combine_rows",
    )(dest_flat, y, top_w_t, h1, final_w.reshape(1, d))


def _reorder_in_columns(w):
    kv0 = 2 * RET_QK_W + 2 * RET_V_W + SWA_Q_W
    kv1 = kv0 + 2 * SWA_KV_W
    return jnp.concatenate([w[:, :kv0], w[:, kv1:], w[:, kv0:kv1]], axis=1)


def _layer(h, tables, p, final_w, final_norm, expert_tile):
    b, s, d = h.shape
    n = b * s
    cr, sr, cs, sa, sb = tables
    x2 = h.reshape(n, d)
    proj = _in_projection(x2, p["attn_norm_w"], _reorder_in_columns(p["w_in"]).astype(BF16))
    proj3 = proj.reshape(b, s, IN_WIDTH)
    t3 = lambda t: t.reshape(b, s, LANES)
    ret = _retention(proj3, t3(cr), t3(sr))
    swa = _sliding_window(proj3, p["sinks"], t3(cs), t3(sa), t3(sb))
    h1, xn2, top_idx, top_w, rank, counts = _post_attention(
        ret.reshape(n, RET_V_W), swa.reshape(n, SWA_Q_W), proj, x2, p["gate_bias"],
        p["w_ret_out"].astype(BF16), p["w_swa_out"].astype(BF16), p["w_o"].astype(BF16),
        p["ffn_norm_w"], p["router_w"], p["router_b"])

    tm = expert_tile
    n_tiles = (n * TOP_K) // tm + N_EXPERTS
    cnt = counts[:, 0].astype(I32)
    padded = ((cnt + tm - 1) // tm) * tm
    ends = jnp.cumsum(padded)
    group_start = ends - padded
    n_active = (ends[-1] // tm).reshape(1)

    dest_flat = _dest_rows(top_idx, rank, group_start).reshape(TOP_K * n)
    xs = _dispatch(dest_flat, group_start + cnt, padded - cnt, n_active, xn2, n_tiles * tm, tm)

    bgu = p["b_gate_up"]
    y = _expert_ffn(padded // tm, xs, p["w_gate_up"], bgu[:, 0::2].reshape(N_EXPERTS, 1, D_FF),
                    bgu[:, 1::2].reshape(N_EXPERTS, 1, D_FF), p["w_down"], p["b_down"].reshape(N_EXPERTS, 1, d), tm)
    out = _combine(dest_flat, y, top_w.T, h1, final_w, final_norm)
    return out.reshape(b, s, d)


def kernel(x, positions, attn_norm_w, w_in, gate_bias, w_ret_out, w_swa_out, w_o, sinks, ffn_norm_w, router_w,
           router_b, w_gate_up, b_gate_up, w_down, b_down, final_norm_w):
    depth = w_in.shape[0]
    stacked = dict(attn_norm_w=attn_norm_w, w_in=w_in, gate_bias=gate_bias, w_ret_out=w_ret_out,
                   w_swa_out=w_swa_out, w_o=w_o, sinks=sinks, ffn_norm_w=ffn_norm_w, router_w=router_w,
                   router_b=router_b, w_gate_up=w_gate_up, b_gate_up=b_gate_up, w_down=w_down, b_down=b_down)
    tables = _rope_tables(positions)
    h = x
    for layer in range(depth):
        p = {name: w[layer] for name, w in stacked.items()}
        h = _layer(h, tables, p, final_norm_w, layer == depth - 1, expert_tile=256)
    return h
```

```python
import functools
import math

import jax
import jax.numpy as jnp
from jax import lax
from jax.experimental import pallas as pl
from jax.experimental.pallas import tpu as pltpu

F32 = jnp.float32
BF16 = jnp.bfloat16
I32 = jnp.int32

D_MODEL = 2048
RMS_EPS = 1e-5
ROPE_THETA = 10000.0
RET_HEADS = 8
RET_DK = 128
RET_DV = 256
RET_CHUNK = 128
SWA_Q_HEADS = 32
SWA_KV_HEADS = 4
SWA_HEAD_DIM = 64
SWA_WINDOW = 128
SWA_BLOCK = 128
N_EXPERTS = 32
TOP_K = 4
D_FF = 2048
SWIGLU_LIMIT = 7.0
SWIGLU_ALPHA = 1.702

RET_QK_W = RET_HEADS * RET_DK
RET_V_W = RET_HEADS * RET_DV
SWA_Q_W = SWA_Q_HEADS * SWA_HEAD_DIM
SWA_KV_W = SWA_KV_HEADS * SWA_HEAD_DIM
GATE_W = 2 * D_MODEL
IN_WIDTH = 2 * RET_QK_W + 2 * RET_V_W + SWA_Q_W + 2 * SWA_KV_W + GATE_W

LANES = 128
VMEM_LIMIT = 56 * 1024 * 1024
NEG_BIG = -1e30

_NT = (((1,), (1,)), ((), ()))
_TN = (((0,), (0,)), ((), ()))


def _cparams(sem):
    return pltpu.CompilerParams(dimension_semantics=sem, vmem_limit_bytes=VMEM_LIMIT)


def _sigmoid(x):
    return 1.0 / (1.0 + jnp.exp(-x))


def _rope_table_kernel(pos_ref, invr_ref, invs_ref, cr_ref, sr_ref, cs_ref, sa_ref, sb_ref):
    pos = pos_ref[...]
    lane = lax.broadcasted_iota(I32, (pos.shape[0], LANES), 1)
    ang_r = pos * invr_ref[...]
    cr_ref[...] = jnp.cos(ang_r)
    sr_ref[...] = jnp.where(lane < RET_DK // 2, -1.0, 1.0) * jnp.sin(ang_r)
    ang_s = pos * invs_ref[...]
    sin_s = jnp.sin(ang_s)
    first_half = (lane % SWA_HEAD_DIM) < SWA_HEAD_DIM // 2
    cs_ref[...] = jnp.cos(ang_s)
    sa_ref[...] = jnp.where(first_half, -sin_s, 0.0)
    sb_ref[...] = jnp.where(first_half, 0.0, sin_s)


def _rope_tables(positions):
    n = positions.size
    pos = positions.reshape(n, 1).astype(F32)
    inv_r = 1.0 / (ROPE_THETA ** (jnp.arange(0, RET_DK, 2, dtype=F32) / RET_DK))
    inv_s = 1.0 / (ROPE_THETA ** (jnp.arange(0, SWA_HEAD_DIM, 2, dtype=F32) / SWA_HEAD_DIM))
    inv_r = jnp.tile(inv_r, LANES // inv_r.size).reshape(1, LANES)
    inv_s = jnp.tile(inv_s, LANES // inv_s.size).reshape(1, LANES)
    t = min(2048, n)
    tab = pl.BlockSpec((t, LANES), lambda i: (i, 0))
    row = pl.BlockSpec((1, LANES), lambda i: (0, 0))
    return pl.pallas_call(
        _rope_table_kernel,
        grid=(n // t,),
        in_specs=[pl.BlockSpec((t, 1), lambda i: (i, 0)), row, row],
        out_specs=[tab] * 5,
        out_shape=[jax.ShapeDtypeStruct((n, LANES), F32)] * 5,
        compiler_params=_cparams(("parallel",)),
        name="rope_tables",
    )(pos, inv_r, inv_s)


def _inproj_kernel(x_ref, nw_ref, w_ref, o_ref, xn_ref):
    @pl.when(pl.program_id(1) == 0)
    def _():
        x = x_ref[...]
        ms = jnp.mean(x * x, axis=-1, keepdims=True)
        xn_ref[...] = (x * lax.rsqrt(ms + RMS_EPS) * nw_ref[...]).astype(BF16)

    o_ref[...] = jnp.dot(xn_ref[...], w_ref[...], preferred_element_type=F32).astype(o_ref.dtype)


def _in_projection(x2, norm_w, w_in_bf16):
    n, d = x2.shape
    width = w_in_bf16.shape[1]
    tm = min(1024, n)
    tn = 1280
    return pl.pallas_call(
        _inproj_kernel,
        grid=(n // tm, width // tn),
        in_specs=[pl.BlockSpec((tm, d), lambda i, j: (i, 0)),
                  pl.BlockSpec((1, d), lambda i, j: (0, 0)),
                  pl.BlockSpec((d, tn), lambda i, j: (0, j))],
        out_specs=pl.BlockSpec((tm, tn), lambda i, j: (i, j)),
        out_shape=jax.ShapeDtypeStruct((n, width), BF16),
        scratch_shapes=[pltpu.VMEM((tm, d), BF16)],
        compiler_params=_cparams(("parallel", "arbitrary")),
        name="in_projection",
    )(x2, norm_w.reshape(1, d), w_in_bf16)


def _retention_kernel(q_ref, k_ref, v_ref, g_ref, cr_ref, sr_ref, o_ref, st_ref):
    @pl.when(pl.program_id(1) == 0)
    def _():
        st_ref[...] = jnp.zeros_like(st_ref)

    c = RET_CHUNK
    cr = cr_ref[...]
    sr = sr_ref[...]
    row = lax.broadcasted_iota(I32, (c, c), 0)
    col = lax.broadcasted_iota(I32, (c, c), 1)
    rel = (row - col).astype(F32)
    n_row = row.astype(F32)
    scale = RET_DK ** -0.5
    for h in range(RET_HEADS):
        log_gamma = math.log1p(-(2.0 ** (-5.0 - h)))
        q = q_ref[:, h * RET_DK:(h + 1) * RET_DK].astype(F32)
        k = k_ref[:, h * RET_DK:(h + 1) * RET_DK].astype(F32)
        qr = q * cr + pltpu.roll(q, RET_DK // 2, 1) * sr
        kr = k * cr + pltpu.roll(k, RET_DK // 2, 1) * sr
        decay = jnp.where(rel >= 0, jnp.exp(log_gamma * jnp.maximum(rel, 0.0)), 0.0) * scale
        s = lax.dot_general(qr.astype(BF16), kr.astype(BF16), _NT, preferred_element_type=F32) * decay
        v = v_ref[:, h * RET_DV:(h + 1) * RET_DV]
        inner = jnp.dot(s.astype(BF16), v, preferred_element_type=F32)
        xi = jnp.exp(log_gamma * (n_row + 1.0))
        st = st_ref[h]
        cross = jnp.dot((qr * xi).astype(BF16), st.astype(BF16), preferred_element_type=F32)
        o = inner + cross
        zeta = jnp.exp(log_gamma * (c - 1.0 - n_row)) * scale
        u = lax.dot_general((kr * zeta).astype(BF16), v, _TN, preferred_element_type=F32)
        st_ref[h] = math.exp(log_gamma * c) * st + u
        o = o * lax.rsqrt(jnp.mean(o * o, axis=-1, keepdims=True) + RMS_EPS)
        g = g_ref[:, h * RET_DV:(h + 1) * RET_DV].astype(F32)
        o_ref[:, h * RET_DV:(h + 1) * RET_DV] = (g * _sigmoid(g) * o).astype(o_ref.dtype)


def _retention(proj3, cr3, sr3):
    b, s, _ = proj3.shape
    c = RET_CHUNK
    return pl.pallas_call(
        _retention_kernel,
        grid=(b, s // c),
        in_specs=[pl.BlockSpec((None, c, RET_QK_W), lambda i, j: (i, j, 0)),
                  pl.BlockSpec((None, c, RET_QK_W), lambda i, j: (i, j, 1)),
                  pl.BlockSpec((None, c, RET_V_W), lambda i, j: (i, j, 1)),
                  pl.BlockSpec((None, c, RET_V_W), lambda i, j: (i, j, 2)),
                  pl.BlockSpec((None, c, LANES), lambda i, j: (i, j, 0)),
                  pl.BlockSpec((None, c, LANES), lambda i, j: (i, j, 0))],
        out_specs=pl.BlockSpec((None, c, RET_V_W), lambda i, j: (i, j, 0)),
        out_shape=jax.ShapeDtypeStruct((b, s, RET_V_W), BF16),
        scratch_shapes=[pltpu.VMEM((RET_HEADS, RET_DK, RET_DV), F32)],
        compiler_params=_cparams(("arbitrary", "arbitrary")),
        name="retention",
    )(proj3, proj3, proj3, proj3, cr3, sr3)


def _swa_kernel(sinks_ref, q_ref, kv_ref, kvp_ref, cs_ref, sa_ref, sb_ref, csp_ref, sap_ref, sbp_ref, o_ref):
    blk = pl.program_id(1)
    wb = SWA_BLOCK
    hd = SWA_HEAD_DIM
    group = SWA_Q_HEADS // SWA_KV_HEADS

    def rope(x, cs, sa, sb):
        return x * cs + pltpu.roll(x, LANES - hd // 2, 1) * sa + pltpu.roll(x, hd // 2, 1) * sb

    def rope_keys(ref, tabs):
        return [rope(ref[:, j * LANES:(j + 1) * LANES].astype(F32), *tabs) for j in range(SWA_KV_W // LANES)]

    cur_tabs = (cs_ref[...], sa_ref[...], sb_ref[...])
    kc = rope_keys(kv_ref, cur_tabs)
    kp = rope_keys(kvp_ref, (csp_ref[...], sap_ref[...], sbp_ref[...]))
    vc = kv_ref[:, SWA_KV_W:2 * SWA_KV_W].astype(F32)
    vp = kvp_ref[:, SWA_KV_W:2 * SWA_KV_W].astype(F32)

    qr = [rope(q_ref[:, j * LANES:(j + 1) * LANES].astype(F32), *cur_tabs) * (hd ** -0.5)
          for j in range(SWA_Q_W // LANES)]

    def head_cols(slabs, h):
        return slabs[h // 2][:, (h % 2) * hd:(h % 2 + 1) * hd]

    cols = group * wb
    kj = lax.broadcasted_iota(I32, (2 * wb, cols), 0)
    qi = lax.broadcasted_iota(I32, (2 * wb, cols), 1) & (wb - 1)
    valid = (kj > qi) & (kj <= qi + SWA_WINDOW) & ((kj >= wb) | (blk > 0))

    for kh in range(SWA_KV_HEADS):
        kband = jnp.concatenate([head_cols(kp, kh), head_cols(kc, kh)], axis=0).astype(BF16)
        vband = jnp.concatenate([vp[:, kh * hd:(kh + 1) * hd], vc[:, kh * hd:(kh + 1) * hd]], axis=0).astype(BF16)
        heads = [kh * group + g for g in range(group)]
        qg = jnp.concatenate([head_cols(qr, h) for h in heads], axis=0).astype(BF16)
        sink = jnp.concatenate([jnp.full((1, wb), sinks_ref[h], F32) for h in heads], axis=1)
        s = lax.dot_general(kband, qg, _NT, preferred_element_type=F32)
        s = jnp.where(valid, s, NEG_BIG)
        m = jnp.maximum(jnp.max(s, axis=0, keepdims=True), sink)
        p = jnp.exp(s - m)
        denom = jnp.sum(p, axis=0, keepdims=True) + jnp.exp(sink - m)
        o = lax.dot_general((p * (1.0 / denom)).astype(BF16), vband, _TN, preferred_element_type=F32)
        for g in range(0, group, 2):
            pair = jnp.concatenate([o[g * wb:(g + 1) * wb], o[(g + 1) * wb:(g + 2) * wb]], axis=1)
            h = heads[g]
            o_ref[:, h * hd:(h + 2) * hd] = pair.astype(o_ref.dtype)


def _sliding_window(proj3, sinks, cs3, sa3, sb3):
    b, s, _ = proj3.shape
    wb = SWA_BLOCK
    q_blk = (2 * RET_QK_W + 2 * RET_V_W) // SWA_Q_W
    kv_blk = (2 * RET_QK_W + 2 * RET_V_W + SWA_Q_W + GATE_W) // (2 * SWA_KV_W)
    prev = lambda j: jnp.maximum(j - 1, 0)
    tab = pl.BlockSpec((None, wb, LANES), lambda i, j: (i, j, 0))
    tab_prev = pl.BlockSpec((None, wb, LANES), lambda i, j: (i, prev(j), 0))
    return pl.pallas_call(
        _swa_kernel,
        grid=(b, s // wb),
        in_specs=[pl.BlockSpec(memory_space=pltpu.SMEM),
                  pl.BlockSpec((None, wb, SWA_Q_W), lambda i, j: (i, j, q_blk)),
                  pl.BlockSpec((None, wb, 2 * SWA_KV_W), lambda i, j: (i, j, kv_blk)),
                  pl.BlockSpec((None, wb, 2 * SWA_KV_W), lambda i, j: (i, prev(j), kv_blk)),
                  tab, tab, tab, tab_prev, tab_prev, tab_prev],
        out_specs=pl.BlockSpec((None, wb, SWA_Q_W), lambda i, j: (i, j, 0)),
        out_shape=jax.ShapeDtypeStruct((b, s, SWA_Q_W), BF16),
        compiler_params=_cparams(("parallel", "parallel")),
        name="sliding_window",
    )(sinks, proj3, proj3, proj3, cs3, sa3, sb3, cs3, sa3, sb3)


def _post_kernel(ret_ref, swa_ref, gl_ref, x_ref, gb_ref, wro_ref, wso_ref, wo_ref, fw_ref, rwt_ref, rb_ref,
                 h1_ref, xn2_ref, idx_ref, tw_ref, rank_ref, cnt_ref, carry_ref):
    @pl.when(pl.program_id(0) == 0)
    def _():
        carry_ref[...] = jnp.zeros_like(carry_ref)

    d = D_MODEL
    rows = x_ref.shape[0]
    n_sub = 2
    for r in range(n_sub):
        rs = slice(r * rows // n_sub, (r + 1) * rows // n_sub)
        a = jnp.dot(ret_ref[rs, :], wro_ref[...], preferred_element_type=F32)
        b = jnp.dot(swa_ref[rs, :], wso_ref[...], preferred_element_type=F32)
        gl = gl_ref[rs, :].astype(F32) + gb_ref[...]
        mix = _sigmoid(gl[:, :d]) * a + _sigmoid(gl[:, d:]) * b
        h1 = x_ref[rs, :] + jnp.dot(mix.astype(BF16), wo_ref[...], preferred_element_type=F32)
        h1_ref[rs, :] = h1
        xn2_ref[rs, :] = h1 * lax.rsqrt(jnp.mean(h1 * h1, axis=-1, keepdims=True) + RMS_EPS) * fw_ref[...]
    xn2 = xn2_ref[...]

    xh = xn2.astype(BF16)
    xl = (xn2 - xh.astype(F32)).astype(BF16)
    rw = rwt_ref[...]
    rh = rw.astype(BF16)
    rl = (rw - rh.astype(F32)).astype(BF16)
    logits = (lax.dot_general(rh, xh, _NT, preferred_element_type=F32)
              + lax.dot_general(rh, xl, _NT, preferred_element_type=F32)
              + lax.dot_general(rl, xh, _NT, preferred_element_type=F32)) + rb_ref[...]
    tm = logits.shape[1]
    e_iota = lax.broadcasted_iota(I32, (N_EXPERTS, tm), 0)
    vals, idxs = [], []
    for _ in range(TOP_K):
        m = jnp.max(logits, axis=0, keepdims=True)
        ix = jnp.min(jnp.where(logits == m, e_iota, N_EXPERTS), axis=0, keepdims=True)
        vals.append(m)
        idxs.append(ix)
        logits = jnp.where(e_iota == ix, -jnp.inf, logits)
    ex = [jnp.exp(v - vals[0]) for v in vals]
    den = ex[0] + ex[1] + ex[2] + ex[3]
    tw_ref[...] = jnp.concatenate([e / den for e in ex], axis=0)
    idx_ref[...] = jnp.concatenate(idxs, axis=0)

    onehot = jnp.zeros((N_EXPERTS, tm), F32)
    for ix in idxs:
        onehot = onehot + (e_iota == ix).astype(F32)
    earlier = (lax.broadcasted_iota(I32, (tm, tm), 0) < lax.broadcasted_iota(I32, (tm, tm), 1)).astype(BF16)
    prefix = jnp.dot(onehot.astype(BF16), earlier, preferred_element_type=F32) + carry_ref[:, 0:1]
    ranks = [jnp.sum(jnp.where(e_iota == ix, prefix, 0.0), axis=0, keepdims=True) for ix in idxs]
    rank_ref[...] = jnp.concatenate(ranks, axis=0).astype(I32)
    carry = carry_ref[...] + jnp.sum(onehot, axis=1, keepdims=True)
    carry_ref[...] = carry
    cnt_ref[...] = carry


def _post_attention(ret2, swa2, proj2, x2, gate_bias, wro, wso, wo, ffn_w, router_w, router_b):
    n, d = x2.shape
    tm = min(256, n)
    gl_blk = (2 * RET_QK_W + 2 * RET_V_W + SWA_Q_W) // GATE_W
    const = lambda shape: pl.BlockSpec(shape, lambda i: (0,) * len(shape), pipeline_mode=pl.Buffered(1))
    rowblk = lambda w: pl.BlockSpec((tm, w), lambda i: (i, 0))
    tokrow = pl.BlockSpec((TOP_K, tm), lambda i: (0, i))
    return pl.pallas_call(
        _post_kernel,
        grid=(n // tm,),
        in_specs=[rowblk(RET_V_W), rowblk(SWA_Q_W),
                  pl.BlockSpec((tm, GATE_W), lambda i: (i, gl_blk)),
                  rowblk(d), const((1, GATE_W)),
                  const((RET_V_W, d)), const((SWA_Q_W, d)), const((d, d)),
                  const((1, d)), const((N_EXPERTS, d)), const((N_EXPERTS, 1))],
        out_specs=[rowblk(d), rowblk(d), tokrow, tokrow, tokrow,
                   pl.BlockSpec((N_EXPERTS, LANES), lambda i: (0, 0))],
        out_shape=[jax.ShapeDtypeStruct((n, d), F32), jax.ShapeDtypeStruct((n, d), F32),
                   jax.ShapeDtypeStruct((TOP_K, n), I32), jax.ShapeDtypeStruct((TOP_K, n), F32),
                   jax.ShapeDtypeStruct((TOP_K, n), I32), jax.ShapeDtypeStruct((N_EXPERTS, LANES), F32)],
        scratch_shapes=[pltpu.VMEM((N_EXPERTS, LANES), F32)],
        compiler_params=_cparams(("arbitrary",)),
        name="post_attention",
    )(ret2, swa2, proj2, x2, gate_bias.reshape(1, GATE_W), wro, wso, wo, ffn_w.reshape(1, d),
      router_w.T, router_b.reshape(N_EXPERTS, 1))


def _dest_kernel(idx_ref, rank_ref, gs_ref, o_ref):
    t = idx_ref.shape[1]
    e_iota = lax.broadcasted_iota(I32, (N_EXPERTS, t), 0)
    gs = gs_ref[...]
    rows = []
    for k in range(TOP_K):
        start = jnp.sum(jnp.where(e_iota == idx_ref[k:k + 1, :], gs, 0), axis=0, keepdims=True)
        rows.append(start + rank_ref[k:k + 1, :])
    o_ref[...] = jnp.concatenate(rows, axis=0)


def _dest_rows(top_idx, rank, group_start):
    n = top_idx.shape[1]
    t = min(2048, n)
    blk = pl.BlockSpec((TOP_K, t), lambda i: (0, i))
    return pl.pallas_call(
        _dest_kernel,
        grid=(n // t,),
        in_specs=[blk, blk, pl.BlockSpec((N_EXPERTS, 1), lambda i: (0, 0))],
        out_specs=blk,
        out_shape=jax.ShapeDtypeStruct((TOP_K, n), I32),
        compiler_params=_cparams(("parallel",)),
        name="dest_rows",
    )(top_idx, rank, group_start.reshape(N_EXPERTS, 1))


def _row_copy(src, s, dst, d, sem):
    return pltpu.make_async_copy(src.at[pl.ds(s, 1), :], dst.at[pl.ds(d, 1), :], sem)


def _dispatch_kernel(dest_ref, pad_lo_ref, pad_n_ref, na_ref, x_ref, o_ref, zero_ref, sem, zsem, *, tm):
    i = pl.program_id(0)
    tq = x_ref.shape[0]
    n = pl.num_programs(0) * tq
    n_tiles = o_ref.shape[0] // tm

    @pl.when(i == 0)
    def _():
        zero_ref[...] = jnp.zeros_like(zero_ref)

        def pad_rows(e, total):
            def one(r, carry):
                _row_copy(zero_ref, 0, o_ref, pad_lo_ref[e] + r, zsem).start()
                return carry
            lax.fori_loop(0, pad_n_ref[e], one, 0)
            return total + pad_n_ref[e]

        n_pad = lax.fori_loop(0, N_EXPERTS, pad_rows, 0)

        def tile_copy(t):
            return pltpu.make_async_copy(zero_ref, o_ref.at[pl.ds(t * tm, tm), :], zsem)

        def start_tile(t, carry):
            tile_copy(t).start()
            return carry

        def drain_tile(t, carry):
            tile_copy(0).wait()
            return carry

        def drain_row(r, carry):
            _row_copy(zero_ref, 0, o_ref, 0, zsem).wait()
            return carry

        lax.fori_loop(na_ref[0], n_tiles, start_tile, 0)
        lax.fori_loop(0, n_pad, drain_row, 0)
        lax.fori_loop(na_ref[0], n_tiles, drain_tile, 0)

    def issue(r, carry):
        for k in range(TOP_K):
            _row_copy(x_ref, r, o_ref, dest_ref[k * n + i * tq + r], sem).start()
        return carry

    lax.fori_loop(0, tq, issue, 0, unroll=8)
    for _ in range(TOP_K):
        pltpu.make_async_copy(x_ref, o_ref.at[pl.ds(0, tq), :], sem).wait()


def _dispatch(dest_flat, pad_lo, pad_n, n_active, xn2, n_rows, tm):
    n, d = xn2.shape
    tq = min(512, n)
    return pl.pallas_call(
        functools.partial(_dispatch_kernel, tm=tm),
        grid_spec=pltpu.PrefetchScalarGridSpec(
            num_scalar_prefetch=4,
            grid=(n // tq,),
            in_specs=[pl.BlockSpec((tq, d), lambda i, *_: (i, 0))],
            out_specs=pl.BlockSpec(memory_space=pl.ANY),
            scratch_shapes=[pltpu.VMEM((tm, d), F32), pltpu.SemaphoreType.DMA(()), pltpu.SemaphoreType.DMA(())],
        ),
        out_shape=jax.ShapeDtypeStruct((n_rows, d), F32),
        compiler_params=_cparams(("arbitrary",)),
        name="dispatch_rows",
    )(dest_flat, pad_lo, pad_n, n_active, xn2)


GU_CHUNK = 512
DN_CHUNK = 256


IDLE_STEP = -1
FILL_STEP = -2


def _work_schedule(tiles_per_expert, n_chunks, n_steps, n_tiles):
    e_ids = jnp.arange(N_EXPERTS, dtype=I32)
    steps_e = jnp.where(e_ids < N_EXPERTS - 1, jnp.maximum(tiles_per_expert, n_chunks), tiles_per_expert)
    end_e = n_chunks + jnp.cumsum(steps_e)
    start_e = end_e - steps_e
    first_tile_e = jnp.cumsum(tiles_per_expert) - tiles_per_expert
    w = jnp.arange(n_steps, dtype=I32)[:, None]
    prologue = w[:, 0] < n_chunks
    in_e = jnp.logical_and(w >= start_e[None, :], w < end_e[None, :])
    local_e = w - start_e[None, :]
    tile_e = jnp.logical_and(in_e, local_e < tiles_per_expert[None, :])
    conv_e = jnp.logical_and(jnp.logical_and(in_e, local_e < n_chunks), e_ids[None, :] < N_EXPERTS - 1)
    has_tile = jnp.any(tile_e, axis=1)
    has_conv = jnp.logical_or(prologue, jnp.any(conv_e, axis=1))
    tile_id = jnp.sum(jnp.where(tile_e, first_tile_e[None, :] + local_e, 0), axis=1)
    cur_expert = jnp.sum(jnp.where(in_e, e_ids[None, :], 0), axis=1)
    cur_expert = jnp.where(w[:, 0] >= end_e[N_EXPERTS - 1], N_EXPERTS - 1, cur_expert)
    n_used = jnp.sum(tiles_per_expert)
    spare = w[:, 0] - end_e[N_EXPERTS - 1]
    fill = jnp.logical_and(spare >= 0, n_used + spare < n_tiles)
    tiles_done = jnp.sum(jnp.clip(w + 1 - start_e[None, :], 0, tiles_per_expert[None, :]), axis=1)
    fills_done = jnp.clip(spare + 1, 0, n_tiles - n_used)
    convs_done = jnp.clip(w[:, 0] + 1, 0, n_chunks) + jnp.sum(
        jnp.where(e_ids[None, :] < N_EXPERTS - 1, jnp.clip(w + 1 - start_e[None, :], 0, n_chunks), 0), axis=1)
    in_blk = jnp.maximum(tiles_done - 1, 0)
    out_blk = jnp.maximum(tiles_done + fills_done - 1, 0)
    conv_lin = jnp.maximum(convs_done - 1, 0)
    tile = jnp.where(has_tile, tile_id, jnp.where(fill, FILL_STEP, IDLE_STEP))
    return (tile.astype(I32), in_blk.astype(I32), out_blk.astype(I32), has_conv.astype(I32),
            (conv_lin // n_chunks).astype(I32), (conv_lin % n_chunks).astype(I32), (cur_expert % 2).astype(I32),
            cur_expert.astype(I32))


def _gate_up_kernel(tile_ref, iblk_ref, oblk_ref, conv_ref, cexp_ref, cchunk_ref, slot_ref, bexp_ref,
                    x_ref, w32_ref, bg_ref, bl_ref, o_ref, wbuf_ref):
    del iblk_ref, oblk_ref, bexp_ref
    step = pl.program_id(0)

    @pl.when(tile_ref[step] == FILL_STEP)
    def _():
        o_ref[...] = jnp.zeros_like(o_ref)

    half = GU_CHUNK // 2
    n_slabs = 2 * D_FF // GU_CHUNK

    @pl.when(conv_ref[step] == 1)
    def _():
        src = lax.broadcasted_iota(I32, (half, half), 0)
        dst = lax.broadcasted_iota(I32, (half, half), 1)
        perm = (src == jnp.where(dst < half // 2, 2 * dst, 2 * (dst - half // 2) + 1)).astype(BF16)
        parts = [jnp.dot(w32_ref[:, j * half:(j + 1) * half].astype(BF16), perm,
                         preferred_element_type=F32).astype(BF16) for j in range(2)]
        cslot = cexp_ref[step] % 2
        c = cchunk_ref[step]
        wbuf_ref[cslot, c] = jnp.concatenate([p[:, :half // 2] for p in parts], axis=1)
        wbuf_ref[cslot, n_slabs + c] = jnp.concatenate([p[:, half // 2:] for p in parts], axis=1)

    @pl.when(tile_ref[step] >= 0)
    def _():
        slot = slot_ref[step]
        x = x_ref[...].astype(BF16)
        per = 2
        for c in range(0, n_slabs, per):
            sl = slice(c * half, (c + per) * half)
            wg = jnp.concatenate([wbuf_ref[slot, c + j] for j in range(per)], axis=1)
            wl = jnp.concatenate([wbuf_ref[slot, n_slabs + c + j] for j in range(per)], axis=1)
            gate = jnp.dot(x, wg, preferred_element_type=F32) + bg_ref[:, sl]
            lin = jnp.dot(x, wl, preferred_element_type=F32) + bl_ref[:, sl]
            gate = jnp.minimum(gate, SWIGLU_LIMIT)
            lin = jnp.clip(lin, -SWIGLU_LIMIT, SWIGLU_LIMIT)
            o_ref[:, sl] = (gate * _sigmoid(SWIGLU_ALPHA * gate) * (lin + 1.0)).astype(o_ref.dtype)


def _down_kernel(tile_ref, iblk_ref, oblk_ref, conv_ref, cexp_ref, cchunk_ref, slot_ref, bexp_ref,
                 a_ref, w32_ref, bd_ref, o_ref, wbuf_ref):
    del iblk_ref, oblk_ref, bexp_ref
    step = pl.program_id(0)

    @pl.when(tile_ref[step] == FILL_STEP)
    def _():
        o_ref[...] = jnp.zeros_like(o_ref)


    @pl.when(conv_ref[step] == 1)
    def _():
        wbuf_ref[cexp_ref[step] % 2, cchunk_ref[step]] = w32_ref[...].astype(BF16)

    @pl.when(tile_ref[step] >= 0)
    def _():
        slot = slot_ref[step]
        acc = bd_ref[...] + jnp.dot(a_ref[:, :DN_CHUNK], wbuf_ref[slot, 0], preferred_element_type=F32)
        for c in range(1, D_FF // DN_CHUNK):
            acc = acc + jnp.dot(a_ref[:, c * DN_CHUNK:(c + 1) * DN_CHUNK], wbuf_ref[slot, c],
                                preferred_element_type=F32)
        o_ref[...] = acc


def _expert_ffn(tiles_per_expert, xs, wgu, bg, bl, wd, bd, tm):
    n_rows, d = xs.shape
    n_tiles = n_rows // tm
    f = wd.shape[1]
    rows_in = lambda w: pl.BlockSpec((tm, w), lambda s, tile, iblk, *_: (iblk[s], 0))
    rows_out = lambda w: pl.BlockSpec((tm, w), lambda s, tile, iblk, oblk, *_: (oblk[s], 0))
    bias = lambda w: pl.BlockSpec((None, 1, w), lambda s, t, i, o, c, ce, cc, sl, bexp: (bexp[s], 0, 0))

    n_chunks = 2 * f // GU_CHUNK
    assert n_chunks == f // DN_CHUNK
    n_steps = n_chunks + n_tiles + n_chunks * N_EXPERTS
    schedule = _work_schedule(tiles_per_expert, n_chunks, n_steps, n_tiles)
    act = pl.pallas_call(
        _gate_up_kernel,
        grid_spec=pltpu.PrefetchScalarGridSpec(
            num_scalar_prefetch=8,
            grid=(n_steps,),
            in_specs=[rows_in(d),
                      pl.BlockSpec((None, d, GU_CHUNK),
                                   lambda s, t, i, o, c, cexp, cchunk, *_: (cexp[s], 0, cchunk[s])),
                      bias(f), bias(f)],
            out_specs=rows_out(f),
            scratch_shapes=[pltpu.VMEM((2, 2 * n_chunks, d, GU_CHUNK // 2), BF16)],
        ),
        out_shape=jax.ShapeDtypeStruct((n_rows, f), BF16),
        compiler_params=_cparams(("arbitrary",)),
        name="expert_gate_up",
    )(*schedule, xs, wgu, bg, bl)

    return pl.pallas_call(
        _down_kernel,
        grid_spec=pltpu.PrefetchScalarGridSpec(
            num_scalar_prefetch=8,
            grid=(n_steps,),
            in_specs=[rows_in(f),
                      pl.BlockSpec((None, DN_CHUNK, d),
                                   lambda s, t, i, o, c, cexp, cchunk, *_: (cexp[s], cchunk[s], 0)),
                      bias(d)],
            out_specs=rows_out(d),
            scratch_shapes=[pltpu.VMEM((2, n_chunks, DN_CHUNK, d), BF16)],
        ),
        out_shape=jax.ShapeDtypeStruct((n_rows, d), F32),
        compiler_params=_cparams(("arbitrary",)),
        name="expert_down",
    )(*schedule, act, wd, bd)


def _combine_kernel(dest_ref, y_ref, tw_ref, h1_ref, fw_ref, o_ref, buf_ref, sem, *, final_norm):
    i = pl.program_id(0)
    steps = pl.num_programs(0)
    tq = h1_ref.shape[0]
    n = steps * tq

    def issue(step, slot):
        def body(r, carry):
            for k in range(TOP_K):
                src = dest_ref[k * n + step * tq + r]
                pltpu.make_async_copy(y_ref.at[pl.ds(src, 1), :], buf_ref.at[slot, k, pl.ds(r, 1), :],
                                      sem.at[slot]).start()
            return carry
        lax.fori_loop(0, tq, body, 0, unroll=8)

    @pl.when(i == 0)
    def _():
        issue(0, 0)

    @pl.when(i + 1 < steps)
    def _():
        issue(i + 1, (i + 1) % 2)

    slot = i % 2
    for k in range(TOP_K):
        pltpu.make_async_copy(y_ref.at[pl.ds(0, tq), :], buf_ref.at[slot, k], sem.at[slot]).wait()

    acc = h1_ref[...]
    moe = tw_ref[:, 0:1] * buf_ref[slot, 0]
    for k in range(1, TOP_K):
        moe = moe + tw_ref[:, k:k + 1] * buf_ref[slot, k]
    acc = acc + moe
    if final_norm:
        acc = acc * lax.rsqrt(jnp.mean(acc * acc, axis=-1, keepdims=True) + RMS_EPS) * fw_ref[...]
    o_ref[...] = acc


def _combine(dest_flat, y, top_w_t, h1, final_w, final_norm):
    n, d = h1.shape
    tq = min(256, n)
    return pl.pallas_call(
        functools.partial(_combine_kernel, final_norm=final_norm),
        grid_spec=pltpu.PrefetchScalarGridSpec(
            num_scalar_prefetch=1,
            grid=(n // tq,),
            in_specs=[pl.BlockSpec(memory_space=pl.ANY),
                      pl.BlockSpec((tq, TOP_K), lambda i, dest: (i, 0)),
                      pl.BlockSpec((tq, d), lambda i, dest: (i, 0)),
                      pl.BlockSpec((1, d), lambda i, dest: (0, 0))],
            out_specs=pl.BlockSpec((tq, d), lambda i, dest: (i, 0)),
            scratch_shapes=[pltpu.VMEM((2, TOP_K, tq, d), F32), pltpu.SemaphoreType.DMA((2,))],
        ),
        out_shape=jax.ShapeDtypeStruct((n, d), F32),
        compiler_params=_cparams(("arbitrary",)),
        name="combine_rows",
    )(dest_flat, y, top_w_t, h1, final_w.reshape(1, d))


def _reorder_in_columns(w):
    kv0 = 2 * RET_QK_W + 2 * RET_V_W + SWA_Q_W
    kv1 = kv0 + 2 * SWA_KV_W
    return jnp.concatenate([w[:, :kv0], w[:, kv1:], w[:, kv0:kv1]], axis=1)


def _layer(h, tables, p, final_w, final_norm, expert_tile):
    b, s, d = h.shape
    n = b * s
    cr, sr, cs, sa, sb = tables
    x2 = h.reshape(n, d)
    proj = _in_projection(x2, p["attn_norm_w"], _reorder_in_columns(p["w_in"]).astype(BF16))
    proj3 = proj.reshape(b, s, IN_WIDTH)
    t3 = lambda t: t.reshape(b, s, LANES)
    ret = _retention(proj3, t3(cr), t3(sr))
    swa = _sliding_window(proj3, p["sinks"], t3(cs), t3(sa), t3(sb))
    h1, xn2, top_idx, top_w, rank, counts = _post_attention(
        ret.reshape(n, RET_V_W), swa.reshape(n, SWA_Q_W), proj, x2, p["gate_bias"],
        p["w_ret_out"].astype(BF16), p["w_swa_out"].astype(BF16), p["w_o"].astype(BF16),
        p["ffn_norm_w"], p["router_w"], p["router_b"])

    tm = expert_tile
    n_tiles = (n * TOP_K) // tm + N_EXPERTS
    cnt = counts[:, 0].astype(I32)
    padded = ((cnt + tm - 1) // tm) * tm
    ends = jnp.cumsum(padded)
    group_start = ends - padded
    n_active = (ends[-1] // tm).reshape(1)

    dest_flat = _dest_rows(top_idx, rank, group_start).reshape(TOP_K * n)
    xs = _dispatch(dest_flat, group_start + cnt, padded - cnt, n_active, xn2, n_tiles * tm, tm)

    bgu = p["b_gate_up"]
    y = _expert_ffn(padded // tm, xs, p["w_gate_up"], bgu[:, 0::2].reshape(N_EXPERTS, 1, D_FF),
                    bgu[:, 1::2].reshape(N_EXPERTS, 1, D_FF), p["w_down"], p["b_down"].reshape(N_EXPERTS, 1, d), tm)
    out = _combine(dest_flat, y, top_w.T, h1, final_w, final_norm)
    return out.reshape(b, s, d)


def kernel(x, positions, attn_norm_w, w_in, gate_bias, w_ret_out, w_swa_out, w_o, sinks, ffn_norm_w, router_w,
           router_b, w_gate_up, b_gate_up, w_down, b_down, final_norm_w):
    depth = w_in.shape[0]
    stacked = dict(attn_norm_w=attn_norm_w, w_in=w_in, gate_bias=gate_bias, w_ret_out=w_ret_out,
                   w_swa_out=w_swa_out, w_o=w_o, sinks=sinks, ffn_norm_w=ffn_norm_w, router_w=router_w,
                   router_b=router_b, w_gate_up=w_gate_up, b_gate_up=b_gate_up, w_down=w_down, b_down=b_down)
    tables = _rope_tables(positions)
    h = x
    for layer in range(depth):
        p = {name: w[layer] for name, w in stacked.items()}
        h = _layer(h, tables, p, final_norm_w, layer == depth - 1, expert_tile=256)
    return h
```

```python
import functools
import math

import jax
import jax.numpy as jnp
from jax import lax
from jax.experimental import pallas as pl
from jax.experimental.pallas import tpu as pltpu

F32 = jnp.float32
BF16 = jnp.bfloat16
I32 = jnp.int32

D_MODEL = 2048
RMS_EPS = 1e-5
ROPE_THETA = 10000.0
RET_HEADS = 8
RET_DK = 128
RET_DV = 256
RET_CHUNK = 128
SWA_Q_HEADS = 32
SWA_KV_HEADS = 4
SWA_HEAD_DIM = 64
SWA_WINDOW = 128
SWA_BLOCK = 128
N_EXPERTS = 32
TOP_K = 4
D_FF = 2048
SWIGLU_LIMIT = 7.0
SWIGLU_ALPHA = 1.702

RET_QK_W = RET_HEADS * RET_DK
RET_V_W = RET_HEADS * RET_DV
SWA_Q_W = SWA_Q_HEADS * SWA_HEAD_DIM
SWA_KV_W = SWA_KV_HEADS * SWA_HEAD_DIM
GATE_W = 2 * D_MODEL
IN_WIDTH = 2 * RET_QK_W + 2 * RET_V_W + SWA_Q_W + 2 * SWA_KV_W + GATE_W

LANES = 128
VMEM_LIMIT = 56 * 1024 * 1024
NEG_BIG = -1e30

_NT = (((1,), (1,)), ((), ()))
_TN = (((0,), (0,)), ((), ()))


def _cparams(sem):
    return pltpu.CompilerParams(dimension_semantics=sem, vmem_limit_bytes=VMEM_LIMIT)


def _sigmoid(x):
    return 1.0 / (1.0 + jnp.exp(-x))


def _rope_table_kernel(pos_ref, invr_ref, invs_ref, cr_ref, sr_ref, cs_ref, sa_ref, sb_ref):
    pos = pos_ref[...]
    lane = lax.broadcasted_iota(I32, (pos.shape[0], LANES), 1)
    ang_r = pos * invr_ref[...]
    cr_ref[...] = jnp.cos(ang_r)
    sr_ref[...] = jnp.where(lane < RET_DK // 2, -1.0, 1.0) * jnp.sin(ang_r)
    ang_s = pos * invs_ref[...]
    sin_s = jnp.sin(ang_s)
    first_half = (lane % SWA_HEAD_DIM) < SWA_HEAD_DIM // 2
    cs_ref[...] = jnp.cos(ang_s)
    sa_ref[...] = jnp.where(first_half, -sin_s, 0.0)
    sb_ref[...] = jnp.where(first_half, 0.0, sin_s)


def _rope_tables(positions):
    n = positions.size
    pos = positions.reshape(n, 1).astype(F32)
    inv_r = 1.0 / (ROPE_THETA ** (jnp.arange(0, RET_DK, 2, dtype=F32) / RET_DK))
    inv_s = 1.0 / (ROPE_THETA ** (jnp.arange(0, SWA_HEAD_DIM, 2, dtype=F32) / SWA_HEAD_DIM))
    inv_r = jnp.tile(inv_r, LANES // inv_r.size).reshape(1, LANES)
    inv_s = jnp.tile(inv_s, LANES // inv_s.size).reshape(1, LANES)
    t = min(2048, n)
    tab = pl.BlockSpec((t, LANES), lambda i: (i, 0))
    row = pl.BlockSpec((1, LANES), lambda i: (0, 0))
    return pl.pallas_call(
        _rope_table_kernel,
        grid=(n // t,),
        in_specs=[pl.BlockSpec((t, 1), lambda i: (i, 0)), row, row],
        out_specs=[tab] * 5,
        out_shape=[jax.ShapeDtypeStruct((n, LANES), F32)] * 5,
        compiler_params=_cparams(("parallel",)),
        name="rope_tables",
    )(pos, inv_r, inv_s)


def _inproj_kernel(x_ref, nw_ref, w_ref, o_ref, xn_ref):
    @pl.when(pl.program_id(1) == 0)
    def _():
        x = x_ref[...]
        ms = jnp.mean(x * x, axis=-1, keepdims=True)
        xn_ref[...] = (x * lax.rsqrt(ms + RMS_EPS) * nw_ref[...]).astype(BF16)

    o_ref[...] = jnp.dot(xn_ref[...], w_ref[...], preferred_element_type=F32).astype(o_ref.dtype)


def _in_projection(x2, norm_w, w_in_bf16):
    n, d = x2.shape
    width = w_in_bf16.shape[1]
    tm = min(1024, n)
    tn = 1280
    return pl.pallas_call(
        _inproj_kernel,
        grid=(n // tm, width // tn),
        in_specs=[pl.BlockSpec((tm, d), lambda i, j: (i, 0)),
                  pl.BlockSpec((1, d), lambda i, j: (0, 0)),
                  pl.BlockSpec((d, tn), lambda i, j: (0, j))],
        out_specs=pl.BlockSpec((tm, tn), lambda i, j: (i, j)),
        out_shape=jax.ShapeDtypeStruct((n, width), BF16),
        scratch_shapes=[pltpu.VMEM((tm, d), BF16)],
        compiler_params=_cparams(("parallel", "arbitrary")),
        name="in_projection",
    )(x2, norm_w.reshape(1, d), w_in_bf16)


def _retention_kernel(q_ref, k_ref, v_ref, g_ref, cr_ref, sr_ref, o_ref, st_ref):
    @pl.when(pl.program_id(1) == 0)
    def _():
        st_ref[...] = jnp.zeros_like(st_ref)

    c = RET_CHUNK
    cr = cr_ref[...]
    sr = sr_ref[...]
    row = lax.broadcasted_iota(I32, (c, c), 0)
    col = lax.broadcasted_iota(I32, (c, c), 1)
    rel = (row - col).astype(F32)
    n_row = row.astype(F32)
    scale = RET_DK ** -0.5
    for h in range(RET_HEADS):
        log_gamma = math.log1p(-(2.0 ** (-5.0 - h)))
        q = q_ref[:, h * RET_DK:(h + 1) * RET_DK].astype(F32)
        k = k_ref[:, h * RET_DK:(h + 1) * RET_DK].astype(F32)
        qr = q * cr + pltpu.roll(q, RET_DK // 2, 1) * sr
        kr = k * cr + pltpu.roll(k, RET_DK // 2, 1) * sr
        decay = jnp.where(rel >= 0, jnp.exp(log_gamma * jnp.maximum(rel, 0.0)), 0.0) * scale
        s = lax.dot_general(qr.astype(BF16), kr.astype(BF16), _NT, preferred_element_type=F32) * decay
        v = v_ref[:, h * RET_DV:(h + 1) * RET_DV]
        inner = jnp.dot(s.astype(BF16), v, preferred_element_type=F32)
        xi = jnp.exp(log_gamma * (n_row + 1.0))
        st = st_ref[h]
        cross = jnp.dot((qr * xi).astype(BF16), st.astype(BF16), preferred_element_type=F32)
        o = inner + cross
        zeta = jnp.exp(log_gamma * (c - 1.0 - n_row)) * scale
        u = lax.dot_general((kr * zeta).astype(BF16), v, _TN, preferred_element_type=F32)
        st_ref[h] = math.exp(log_gamma * c) * st + u
        o = o * lax.rsqrt(jnp.mean(o * o, axis=-1, keepdims=True) + RMS_EPS)
        g = g_ref[:, h * RET_DV:(h + 1) * RET_DV].astype(F32)
        o_ref[:, h * RET_DV:(h + 1) * RET_DV] = (g * _sigmoid(g) * o).astype(o_ref.dtype)


def _retention(proj3, cr3, sr3):
    b, s, _ = proj3.shape
    c = RET_CHUNK
    return pl.pallas_call(
        _retention_kernel,
        grid=(b, s // c),
        in_specs=[pl.BlockSpec((None, c, RET_QK_W), lambda i, j: (i, j, 0)),
                  pl.BlockSpec((None, c, RET_QK_W), lambda i, j: (i, j, 1)),
                  pl.BlockSpec((None, c, RET_V_W), lambda i, j: (i, j, 1)),
                  pl.BlockSpec((None, c, RET_V_W), lambda i, j: (i, j, 2)),
                  pl.BlockSpec((None, c, LANES), lambda i, j: (i, j, 0)),
                  pl.BlockSpec((None, c, LANES), lambda i, j: (i, j, 0))],
        out_specs=pl.BlockSpec((None, c, RET_V_W), lambda i, j: (i, j, 0)),
        out_shape=jax.ShapeDtypeStruct((b, s, RET_V_W), BF16),
        scratch_shapes=[pltpu.VMEM((RET_HEADS, RET_DK, RET_DV), F32)],
        compiler_params=_cparams(("arbitrary", "arbitrary")),
        name="retention",
    )(proj3, proj3, proj3, proj3, cr3, sr3)


def _swa_kernel(sinks_ref, q_ref, kv_ref, kvp_ref, cs_ref, sa_ref, sb_ref, csp_ref, sap_ref, sbp_ref, o_ref):
    blk = pl.program_id(1)
    wb = SWA_BLOCK
    hd = SWA_HEAD_DIM
    group = SWA_Q_HEADS // SWA_KV_HEADS

    def rope(x, cs, sa, sb):
        return x * cs + pltpu.roll(x, LANES - hd // 2, 1) * sa + pltpu.roll(x, hd // 2, 1) * sb

    def rope_keys(ref, tabs):
        return [rope(ref[:, j * LANES:(j + 1) * LANES].astype(F32), *tabs) for j in range(SWA_KV_W // LANES)]

    cur_tabs = (cs_ref[...], sa_ref[...], sb_ref[...])
    kc = rope_keys(kv_ref, cur_tabs)
    kp = rope_keys(kvp_ref, (csp_ref[...], sap_ref[...], sbp_ref[...]))
    vc = kv_ref[:, SWA_KV_W:2 * SWA_KV_W].astype(F32)
    vp = kvp_ref[:, SWA_KV_W:2 * SWA_KV_W].astype(F32)

    qr = [rope(q_ref[:, j * LANES:(j + 1) * LANES].astype(F32), *cur_tabs) * (hd ** -0.5)
          for j in range(SWA_Q_W // LANES)]

    def head_cols(slabs, h):
        return slabs[h // 2][:, (h % 2) * hd:(h % 2 + 1) * hd]

    cols = group * wb
    kj = lax.broadcasted_iota(I32, (2 * wb, cols), 0)
    qi = lax.broadcasted_iota(I32, (2 * wb, cols), 1) & (wb - 1)
    valid = (kj > qi) & (kj <= qi + SWA_WINDOW) & ((kj >= wb) | (blk > 0))

    for kh in range(SWA_KV_HEADS):
        kband = jnp.concatenate([head_cols(kp, kh), head_cols(kc, kh)], axis=0).astype(BF16)
        vband = jnp.concatenate([vp[:, kh * hd:(kh + 1) * hd], vc[:, kh * hd:(kh + 1) * hd]], axis=0).astype(BF16)
        heads = [kh * group + g for g in range(group)]
        qg = jnp.concatenate([head_cols(qr, h) for h in heads], axis=0).astype(BF16)
        sink = jnp.concatenate([jnp.full((1, wb), sinks_ref[h], F32) for h in heads], axis=1)
        s = lax.dot_general(kband, qg, _NT, preferred_element_type=F32)
        s = jnp.where(valid, s, NEG_BIG)
        m = jnp.maximum(jnp.max(s, axis=0, keepdims=True), sink)
        p = jnp.exp(s - m)
        denom = jnp.sum(p, axis=0, keepdims=True) + jnp.exp(sink - m)
        o = lax.dot_general((p * (1.0 / denom)).astype(BF16), vband, _TN, preferred_element_type=F32)
        for g in range(0, group, 2):
            pair = jnp.concatenate([o[g * wb:(g + 1) * wb], o[(g + 1) * wb:(g + 2) * wb]], axis=1)
            h = heads[g]
            o_ref[:, h * hd:(h + 2) * hd] = pair.astype(o_ref.dtype)


def _sliding_window(proj3, sinks, cs3, sa3, sb3):
    b, s, _ = proj3.shape
    wb = SWA_BLOCK
    q_blk = (2 * RET_QK_W + 2 * RET_V_W) // SWA_Q_W
    kv_blk = (2 * RET_QK_W + 2 * RET_V_W + SWA_Q_W + GATE_W) // (2 * SWA_KV_W)
    prev = lambda j: jnp.maximum(j - 1, 0)
    tab = pl.BlockSpec((None, wb, LANES), lambda i, j: (i, j, 0))
    tab_prev = pl.BlockSpec((None, wb, LANES), lambda i, j: (i, prev(j), 0))
    return pl.pallas_call(
        _swa_kernel,
        grid=(b, s // wb),
        in_specs=[pl.BlockSpec(memory_space=pltpu.SMEM),
                  pl.BlockSpec((None, wb, SWA_Q_W), lambda i, j: (i, j, q_blk)),
                  pl.BlockSpec((None, wb, 2 * SWA_KV_W), lambda i, j: (i, j, kv_blk)),
                  pl.BlockSpec((None, wb, 2 * SWA_KV_W), lambda i, j: (i, prev(j), kv_blk)),
                  tab, tab, tab, tab_prev, tab_prev, tab_prev],
        out_specs=pl.BlockSpec((None, wb, SWA_Q_W), lambda i, j: (i, j, 0)),
        out_shape=jax.ShapeDtypeStruct((b, s, SWA_Q_W), BF16),
        compiler_params=_cparams(("parallel", "parallel")),
        name="sliding_window",
    )(sinks, proj3, proj3, proj3, cs3, sa3, sb3, cs3, sa3, sb3)


def _post_kernel(ret_ref, swa_ref, gl_ref, x_ref, gb_ref, wro_ref, wso_ref, wo_ref, fw_ref, rwt_ref, rb_ref,
                 h1_ref, xn2_ref, idx_ref, tw_ref, rank_ref, cnt_ref, carry_ref):
    @pl.when(pl.program_id(0) == 0)
    def _():
        carry_ref[...] = jnp.zeros_like(carry_ref)

    d = D_MODEL
    rows = x_ref.shape[0]
    n_sub = 2
    for r in range(n_sub):
        rs = slice(r * rows // n_sub, (r + 1) * rows // n_sub)
        a = jnp.dot(ret_ref[rs, :], wro_ref[...], preferred_element_type=F32)
        b = jnp.dot(swa_ref[rs, :], wso_ref[...], preferred_element_type=F32)
        gl = gl_ref[rs, :].astype(F32) + gb_ref[...]
        mix = _sigmoid(gl[:, :d]) * a + _sigmoid(gl[:, d:]) * b
        h1 = x_ref[rs, :] + jnp.dot(mix.astype(BF16), wo_ref[...], preferred_element_type=F32)
        h1_ref[rs, :] = h1
        xn2_ref[rs, :] = h1 * lax.rsqrt(jnp.mean(h1 * h1, axis=-1, keepdims=True) + RMS_EPS) * fw_ref[...]
    xn2 = xn2_ref[...]

    xh = xn2.astype(BF16)
    xl = (xn2 - xh.astype(F32)).astype(BF16)
    rw = rwt_ref[...]
    rh = rw.astype(BF16)
    rl = (rw - rh.astype(F32)).astype(BF16)
    logits = (lax.dot_general(rh, xh, _NT, preferred_element_type=F32)
              + lax.dot_general(rh, xl, _NT, preferred_element_type=F32)
              + lax.dot_general(rl, xh, _NT, preferred_element_type=F32)) + rb_ref[...]
    tm = logits.shape[1]
    e_iota = lax.broadcasted_iota(I32, (N_EXPERTS, tm), 0)
    vals, idxs = [], []
    for _ in range(TOP_K):
        m = jnp.max(logits, axis=0, keepdims=True)
        ix = jnp.min(jnp.where(logits == m, e_iota, N_EXPERTS), axis=0, keepdims=True)
        vals.append(m)
        idxs.append(ix)
        logits = jnp.where(e_iota == ix, -jnp.inf, logits)
    ex = [jnp.exp(v - vals[0]) for v in vals]
    den = ex[0] + ex[1] + ex[2] + ex[3]
    tw_ref[...] = jnp.concatenate([e / den for e in ex], axis=0)
    idx_ref[...] = jnp.concatenate(idxs, axis=0)

    onehot = jnp.zeros((N_EXPERTS, tm), F32)
    for ix in idxs:
        onehot = onehot + (e_iota == ix).astype(F32)
    earlier = (lax.broadcasted_iota(I32, (tm, tm), 0) < lax.broadcasted_iota(I32, (tm, tm), 1)).astype(BF16)
    prefix = jnp.dot(onehot.astype(BF16), earlier, preferred_element_type=F32) + carry_ref[:, 0:1]
    ranks = [jnp.sum(jnp.where(e_iota == ix, prefix, 0.0), axis=0, keepdims=True) for ix in idxs]
    rank_ref[...] = jnp.concatenate(ranks, axis=0).astype(I32)
    carry = carry_ref[...] + jnp.sum(onehot, axis=1, keepdims=True)
    carry_ref[...] = carry
    cnt_ref[...] = carry


def _post_attention(ret2, swa2, proj2, x2, gate_bias, wro, wso, wo, ffn_w, router_w, router_b):
    n, d = x2.shape
    tm = min(256, n)
    gl_blk = (2 * RET_QK_W + 2 * RET_V_W + SWA_Q_W) // GATE_W
    const = lambda shape: pl.BlockSpec(shape, lambda i: (0,) * len(shape), pipeline_mode=pl.Buffered(1))
    rowblk = lambda w: pl.BlockSpec((tm, w), lambda i: (i, 0))
    tokrow = pl.BlockSpec((TOP_K, tm), lambda i: (0, i))
    return pl.pallas_call(
        _post_kernel,
        grid=(n // tm,),
        in_specs=[rowblk(RET_V_W), rowblk(SWA_Q_W),
                  pl.BlockSpec((tm, GATE_W), lambda i: (i, gl_blk)),
                  rowblk(d), const((1, GATE_W)),
                  const((RET_V_W, d)), const((SWA_Q_W, d)), const((d, d)),
                  const((1, d)), const((N_EXPERTS, d)), const((N_EXPERTS, 1))],
        out_specs=[rowblk(d), rowblk(d), tokrow, tokrow, tokrow,
                   pl.BlockSpec((N_EXPERTS, LANES), lambda i: (0, 0))],
        out_shape=[jax.ShapeDtypeStruct((n, d), F32), jax.ShapeDtypeStruct((n, d), F32),
                   jax.ShapeDtypeStruct((TOP_K, n), I32), jax.ShapeDtypeStruct((TOP_K, n), F32),
                   jax.ShapeDtypeStruct((TOP_K, n), I32), jax.ShapeDtypeStruct((N_EXPERTS, LANES), F32)],
        scratch_shapes=[pltpu.VMEM((N_EXPERTS, LANES), F32)],
        compiler_params=_cparams(("arbitrary",)),
        name="post_attention",
    )(ret2, swa2, proj2, x2, gate_bias.reshape(1, GATE_W), wro, wso, wo, ffn_w.reshape(1, d),
      router_w.T, router_b.reshape(N_EXPERTS, 1))


def _dest_kernel(idx_ref, rank_ref, gs_ref, o_ref):
    t = idx_ref.shape[1]
    e_iota = lax.broadcasted_iota(I32, (N_EXPERTS, t), 0)
    gs = gs_ref[...]
    rows = []
    for k in range(TOP_K):
        start = jnp.sum(jnp.where(e_iota == idx_ref[k:k + 1, :], gs, 0), axis=0, keepdims=True)
        rows.append(start + rank_ref[k:k + 1, :])
    o_ref[...] = jnp.concatenate(rows, axis=0)


def _dest_rows(top_idx, rank, group_start):
    n = top_idx.shape[1]
    t = min(2048, n)
    blk = pl.BlockSpec((TOP_K, t), lambda i: (0, i))
    return pl.pallas_call(
        _dest_kernel,
        grid=(n // t,),
        in_specs=[blk, blk, pl.BlockSpec((N_EXPERTS, 1), lambda i: (0, 0))],
        out_specs=blk,
        out_shape=jax.ShapeDtypeStruct((TOP_K, n), I32),
        compiler_params=_cparams(("parallel",)),
        name="dest_rows",
    )(top_idx, rank, group_start.reshape(N_EXPERTS, 1))


def _row_copy(src, s, dst, d, sem):
    return pltpu.make_async_copy(src.at[pl.ds(s, 1), :], dst.at[pl.ds(d, 1), :], sem)


def _dispatch_kernel(dest_ref, pad_lo_ref, pad_n_ref, na_ref, x_ref, o_ref, zero_ref, sem, zsem, *, tm):
    i = pl.program_id(0)
    tq = x_ref.shape[0]
    n = pl.num_programs(0) * tq
    n_tiles = o_ref.shape[0] // tm

    @pl.when(i == 0)
    def _():
        zero_ref[...] = jnp.zeros_like(zero_ref)

        def pad_rows(e, total):
            def one(r, carry):
                _row_copy(zero_ref, 0, o_ref, pad_lo_ref[e] + r, zsem).start()
                return carry
            lax.fori_loop(0, pad_n_ref[e], one, 0)
            return total + pad_n_ref[e]

        n_pad = lax.fori_loop(0, N_EXPERTS, pad_rows, 0)

        def tile_copy(t):
            return pltpu.make_async_copy(zero_ref, o_ref.at[pl.ds(t * tm, tm), :], zsem)

        def start_tile(t, carry):
            tile_copy(t).start()
            return carry

        def drain_tile(t, carry):
            tile_copy(0).wait()
            return carry

        def drain_row(r, carry):
            _row_copy(zero_ref, 0, o_ref, 0, zsem).wait()
            return carry

        lax.fori_loop(na_ref[0], n_tiles, start_tile, 0)
        lax.fori_loop(0, n_pad, drain_row, 0)
        lax.fori_loop(na_ref[0], n_tiles, drain_tile, 0)

    def issue(r, carry):
        for k in range(TOP_K):
            _row_copy(x_ref, r, o_ref, dest_ref[k * n + i * tq + r], sem).start()
        return carry

    lax.fori_loop(0, tq, issue, 0, unroll=8)
    for _ in range(TOP_K):
        pltpu.make_async_copy(x_ref, o_ref.at[pl.ds(0, tq), :], sem).wait()


def _dispatch(dest_flat, pad_lo, pad_n, n_active, xn2, n_rows, tm):
    n, d = xn2.shape
    tq = min(512, n)
    return pl.pallas_call(
        functools.partial(_dispatch_kernel, tm=tm),
        grid_spec=pltpu.PrefetchScalarGridSpec(
            num_scalar_prefetch=4,
            grid=(n // tq,),
            in_specs=[pl.BlockSpec((tq, d), lambda i, *_: (i, 0))],
            out_specs=pl.BlockSpec(memory_space=pl.ANY),
            scratch_shapes=[pltpu.VMEM((tm, d), F32), pltpu.SemaphoreType.DMA(()), pltpu.SemaphoreType.DMA(())],
        ),
        out_shape=jax.ShapeDtypeStruct((n_rows, d), F32),
        compiler_params=_cparams(("arbitrary",)),
        name="dispatch_rows",
    )(dest_flat, pad_lo, pad_n, n_active, xn2)


GU_CHUNK = 512
DN_CHUNK = 256


IDLE_STEP = -1
FILL_STEP = -2


def _work_schedule(tiles_per_expert, n_chunks, n_steps, n_tiles):
    e_ids = jnp.arange(N_EXPERTS, dtype=I32)
    steps_e = jnp.where(e_ids < N_EXPERTS - 1, jnp.maximum(tiles_per_expert, n_chunks), tiles_per_expert)
    end_e = n_chunks + jnp.cumsum(steps_e)
    start_e = end_e - steps_e
    first_tile_e = jnp.cumsum(tiles_per_expert) - tiles_per_expert
    w = jnp.arange(n_steps, dtype=I32)[:, None]
    prologue = w[:, 0] < n_chunks
    in_e = jnp.logical_and(w >= start_e[None, :], w < end_e[None, :])
    local_e = w - start_e[None, :]
    tile_e = jnp.logical_and(in_e, local_e < tiles_per_expert[None, :])
    conv_e = jnp.logical_and(jnp.logical_and(in_e, local_e < n_chunks), e_ids[None, :] < N_EXPERTS - 1)
    has_tile = jnp.any(tile_e, axis=1)
    has_conv = jnp.logical_or(prologue, jnp.any(conv_e, axis=1))
    tile_id = jnp.sum(jnp.where(tile_e, first_tile_e[None, :] + local_e, 0), axis=1)
    cur_expert = jnp.sum(jnp.where(in_e, e_ids[None, :], 0), axis=1)
    cur_expert = jnp.where(w[:, 0] >= end_e[N_EXPERTS - 1], N_EXPERTS - 1, cur_expert)
    n_used = jnp.sum(tiles_per_expert)
    spare = w[:, 0] - end_e[N_EXPERTS - 1]
    fill = jnp.logical_and(spare >= 0, n_used + spare < n_tiles)
    tiles_done = jnp.sum(jnp.clip(w + 1 - start_e[None, :], 0, tiles_per_expert[None, :]), axis=1)
    fills_done = jnp.clip(spare + 1, 0, n_tiles - n_used)
    convs_done = jnp.clip(w[:, 0] + 1, 0, n_chunks) + jnp.sum(
        jnp.where(e_ids[None, :] < N_EXPERTS - 1, jnp.clip(w + 1 - start_e[None, :], 0, n_chunks), 0), axis=1)
    in_blk = jnp.maximum(tiles_done - 1, 0)
    out_blk = jnp.maximum(tiles_done + fills_done - 1, 0)
    conv_lin = jnp.maximum(convs_done - 1, 0)
    tile = jnp.where(has_tile, tile_id, jnp.where(fill, FILL_STEP, IDLE_STEP))
    steps_needed = end_e[N_EXPERTS - 1] + n_tiles - n_used
    return steps_needed.astype(I32), (
        tile.astype(I32), in_blk.astype(I32), out_blk.astype(I32), has_conv.astype(I32),
        (conv_lin // n_chunks).astype(I32), (conv_lin % n_chunks).astype(I32), (cur_expert % 2).astype(I32),
        cur_expert.astype(I32))


def _gate_up_kernel(tile_ref, iblk_ref, oblk_ref, conv_ref, cexp_ref, cchunk_ref, slot_ref, bexp_ref,
                    x_ref, w32_ref, bg_ref, bl_ref, o_ref, wbuf_ref):
    del iblk_ref, oblk_ref, bexp_ref
    step = pl.program_id(0)

    @pl.when(tile_ref[step] == FILL_STEP)
    def _():
        o_ref[...] = jnp.zeros_like(o_ref)

    half = GU_CHUNK // 2
    n_slabs = 2 * D_FF // GU_CHUNK

    @pl.when(conv_ref[step] == 1)
    def _():
        src = lax.broadcasted_iota(I32, (half, half), 0)
        dst = lax.broadcasted_iota(I32, (half, half), 1)
        perm = (src == jnp.where(dst < half // 2, 2 * dst, 2 * (dst - half // 2) + 1)).astype(BF16)
        parts = [jnp.dot(w32_ref[:, j * half:(j + 1) * half].astype(BF16), perm,
                         preferred_element_type=F32).astype(BF16) for j in range(2)]
        cslot = cexp_ref[step] % 2
        c = cchunk_ref[step]
        wbuf_ref[cslot, c] = jnp.concatenate([p[:, :half // 2] for p in parts], axis=1)
        wbuf_ref[cslot, n_slabs + c] = jnp.concatenate([p[:, half // 2:] for p in parts], axis=1)

    @pl.when(tile_ref[step] >= 0)
    def _():
        slot = slot_ref[step]
        x = x_ref[...].astype(BF16)
        per = 2
        for c in range(0, n_slabs, per):
            sl = slice(c * half, (c + per) * half)
            wg = jnp.concatenate([wbuf_ref[slot, c + j] for j in range(per)], axis=1)
            wl = jnp.concatenate([wbuf_ref[slot, n_slabs + c + j] for j in range(per)], axis=1)
            gate = jnp.dot(x, wg, preferred_element_type=F32) + bg_ref[:, sl]
            lin = jnp.dot(x, wl, preferred_element_type=F32) + bl_ref[:, sl]
            gate = jnp.minimum(gate, SWIGLU_LIMIT)
            lin = jnp.clip(lin, -SWIGLU_LIMIT, SWIGLU_LIMIT)
            o_ref[:, sl] = (gate * _sigmoid(SWIGLU_ALPHA * gate) * (lin + 1.0)).astype(o_ref.dtype)


def _down_kernel(tile_ref, iblk_ref, oblk_ref, conv_ref, cexp_ref, cchunk_ref, slot_ref, bexp_ref,
                 a_ref, w32_ref, bd_ref, o_ref, wbuf_ref):
    del iblk_ref, oblk_ref, bexp_ref
    step = pl.program_id(0)

    @pl.when(tile_ref[step] == FILL_STEP)
    def _():
        o_ref[...] = jnp.zeros_like(o_ref)


    @pl.when(conv_ref[step] == 1)
    def _():
        wbuf_ref[cexp_ref[step] % 2, cchunk_ref[step]] = w32_ref[...].astype(BF16)

    @pl.when(tile_ref[step] >= 0)
    def _():
        slot = slot_ref[step]
        acc = bd_ref[...] + jnp.dot(a_ref[:, :DN_CHUNK], wbuf_ref[slot, 0], preferred_element_type=F32)
        for c in range(1, D_FF // DN_CHUNK):
            acc = acc + jnp.dot(a_ref[:, c * DN_CHUNK:(c + 1) * DN_CHUNK], wbuf_ref[slot, c],
                                preferred_element_type=F32)
        o_ref[...] = acc


def _expert_ffn(tiles_per_expert, xs, wgu, bg, bl, wd, bd, tm):
    n_rows, d = xs.shape
    n_tiles = n_rows // tm
    f = wd.shape[1]
    rows_in = lambda w: pl.BlockSpec((tm, w), lambda s, tile, iblk, *_: (iblk[s], 0))
    rows_out = lambda w: pl.BlockSpec((tm, w), lambda s, tile, iblk, oblk, *_: (oblk[s], 0))
    bias = lambda w: pl.BlockSpec((None, 1, w), lambda s, *p: (p[7][s], 0, 0))

    n_chunks = 2 * f // GU_CHUNK
    assert n_chunks == f // DN_CHUNK
    n_steps = n_chunks + n_tiles + n_chunks * N_EXPERTS
    steps_needed, schedule = _work_schedule(tiles_per_expert, n_chunks, n_steps, n_tiles)
    act = pl.pallas_call(
        _gate_up_kernel,
        grid_spec=pltpu.PrefetchScalarGridSpec(
            num_scalar_prefetch=8,
            grid=(steps_needed,),
            in_specs=[rows_in(d),
                      pl.BlockSpec((None, d, GU_CHUNK),
                                   lambda s, t, i, o, c, cexp, cchunk, *_: (cexp[s], 0, cchunk[s])),
                      bias(f), bias(f)],
            out_specs=rows_out(f),
            scratch_shapes=[pltpu.VMEM((2, 2 * n_chunks, d, GU_CHUNK // 2), BF16)],
        ),
        out_shape=jax.ShapeDtypeStruct((n_rows, f), BF16),
        compiler_params=_cparams(("arbitrary",)),
        name="expert_gate_up",
    )(*schedule, xs, wgu, bg, bl)

    return pl.pallas_call(
        _down_kernel,
        grid_spec=pltpu.PrefetchScalarGridSpec(
            num_scalar_prefetch=8,
            grid=(steps_needed,),
            in_specs=[rows_in(f),
                      pl.BlockSpec((None, DN_CHUNK, d),
                                   lambda s, t, i, o, c, cexp, cchunk, *_: (cexp[s], cchunk[s], 0)),
                      bias(d)],
            out_specs=rows_out(d),
            scratch_shapes=[pltpu.VMEM((2, n_chunks, DN_CHUNK, d), BF16)],
        ),
        out_shape=jax.ShapeDtypeStruct((n_rows, d), F32),
        compiler_params=_cparams(("arbitrary",)),
        name="expert_down",
    )(*schedule, act, wd, bd)


def _combine_kernel(dest_ref, y_ref, tw_ref, h1_ref, fw_ref, o_ref, buf_ref, sem, *, final_norm):
    i = pl.program_id(0)
    steps = pl.num_programs(0)
    tq = h1_ref.shape[0]
    n = steps * tq

    def issue(step, slot):
        def body(r, carry):
            for k in range(TOP_K):
                src = dest_ref[k * n + step * tq + r]
                pltpu.make_async_copy(y_ref.at[pl.ds(src, 1), :], buf_ref.at[slot, k, pl.ds(r, 1), :],
                                      sem.at[slot]).start()
            return carry
        lax.fori_loop(0, tq, body, 0, unroll=8)

    @pl.when(i == 0)
    def _():
        issue(0, 0)

    @pl.when(i + 1 < steps)
    def _():
        issue(i + 1, (i + 1) % 2)

    slot = i % 2
    for k in range(TOP_K):
        pltpu.make_async_copy(y_ref.at[pl.ds(0, tq), :], buf_ref.at[slot, k], sem.at[slot]).wait()

    acc = h1_ref[...]
    moe = tw_ref[:, 0:1] * buf_ref[slot, 0]
    for k in range(1, TOP_K):
        moe = moe + tw_ref[:, k:k + 1] * buf_ref[slot, k]
    acc = acc + moe
    if final_norm:
        acc = acc * lax.rsqrt(jnp.mean(acc * acc, axis=-1, keepdims=True) + RMS_EPS) * fw_ref[...]
    o_ref[...] = acc


def _combine(dest_flat, y, top_w_t, h1, final_w, final_norm):
    n, d = h1.shape
    tq = min(256, n)
    return pl.pallas_call(
        functools.partial(_combine_kernel, final_norm=final_norm),
        grid_spec=pltpu.PrefetchScalarGridSpec(
            num_scalar_prefetch=1,
            grid=(n // tq,),
            in_specs=[pl.BlockSpec(memory_space=pl.ANY),
                      pl.BlockSpec((tq, TOP_K), lambda i, dest: (i, 0)),
                      pl.BlockSpec((tq, d), lambda i, dest: (i, 0)),
                      pl.BlockSpec((1, d), lambda i, dest: (0, 0))],
            out_specs=pl.BlockSpec((tq, d), lambda i, dest: (i, 0)),
            scratch_shapes=[pltpu.VMEM((2, TOP_K, tq, d), F32), pltpu.SemaphoreType.DMA((2,))],
        ),
        out_shape=jax.ShapeDtypeStruct((n, d), F32),
        compiler_params=_cparams(("arbitrary",)),
        name="combine_rows",
    )(dest_flat, y, top_w_t, h1, final_w.reshape(1, d))


def _cast_kernel(w_ref, o_ref):
    o_ref[...] = w_ref[...].astype(o_ref.dtype)


def _cast_in_weights(w):
    d, width = w.shape
    tn = 2 * SWA_KV_W
    kv_blk = (2 * RET_QK_W + 2 * RET_V_W + SWA_Q_W) // tn
    n_blk = width // tn

    def source_block(j):
        return jnp.where(j < kv_blk, j, jnp.where(j < n_blk - 1, j + 1, kv_blk))

    return pl.pallas_call(
        _cast_kernel,
        grid=(n_blk,),
        in_specs=[pl.BlockSpec((d, tn), lambda j: (0, source_block(j)))],
        out_specs=pl.BlockSpec((d, tn), lambda j: (0, j)),
        out_shape=jax.ShapeDtypeStruct((d, width), BF16),
        compiler_params=_cparams(("parallel",)),
        name="cast_in_weights",
    )(w)


def _layer(h, tables, p, final_w, final_norm, expert_tile):
    b, s, d = h.shape
    n = b * s
    cr, sr, cs, sa, sb = tables
    x2 = h.reshape(n, d)
    proj = _in_projection(x2, p["attn_norm_w"], _cast_in_weights(p["w_in"]))
    proj3 = proj.reshape(b, s, IN_WIDTH)
    t3 = lambda t: t.reshape(b, s, LANES)
    ret = _retention(proj3, t3(cr), t3(sr))
    swa = _sliding_window(proj3, p["sinks"], t3(cs), t3(sa), t3(sb))
    h1, xn2, top_idx, top_w, rank, counts = _post_attention(
        ret.reshape(n, RET_V_W), swa.reshape(n, SWA_Q_W), proj, x2, p["gate_bias"],
        p["w_ret_out"].astype(BF16), p["w_swa_out"].astype(BF16), p["w_o"].astype(BF16),
        p["ffn_norm_w"], p["router_w"], p["router_b"])

    tm = expert_tile
    n_tiles = (n * TOP_K) // tm + N_EXPERTS
    cnt = counts[:, 0].astype(I32)
    padded = ((cnt + tm - 1) // tm) * tm
    ends = jnp.cumsum(padded)
    group_start = ends - padded
    n_active = (ends[-1] // tm).reshape(1)

    dest_flat = _dest_rows(top_idx, rank, group_start).reshape(TOP_K * n)
    xs = _dispatch(dest_flat, group_start + cnt, padded - cnt, n_active, xn2, n_tiles * tm, tm)

    bgu = p["b_gate_up"]
    y = _expert_ffn(padded // tm, xs, p["w_gate_up"], bgu[:, 0::2].reshape(N_EXPERTS, 1, D_FF),
                    bgu[:, 1::2].reshape(N_EXPERTS, 1, D_FF), p["w_down"], p["b_down"].reshape(N_EXPERTS, 1, d), tm)
    out = _combine(dest_flat, y, top_w.T, h1, final_w, final_norm)
    return out.reshape(b, s, d)


def kernel(x, positions, attn_norm_w, w_in, gate_bias, w_ret_out, w_swa_out, w_o, sinks, ffn_norm_w, router_w,
           router_b, w_gate_up, b_gate_up, w_down, b_down, final_norm_w):
    depth = w_in.shape[0]
    stacked = dict(attn_norm_w=attn_norm_w, w_in=w_in, gate_bias=gate_bias, w_ret_out=w_ret_out,
                   w_swa_out=w_swa_out, w_o=w_o, sinks=sinks, ffn_norm_w=ffn_norm_w, router_w=router_w,
                   router_b=router_b, w_gate_up=w_gate_up, b_gate_up=b_gate_up, w_down=w_down, b_down=b_down)
    tables = _rope_tables(positions)
    h = x
    for layer in range(depth):
        p = {name: w[layer] for name, w in stacked.items()}
        h = _layer(h, tables, p, final_norm_w, layer == depth - 1, expert_tile=256)
    return h
```

```python
import functools
import math

import jax
import jax.numpy as jnp
from jax import lax
from jax.experimental import pallas as pl
from jax.experimental.pallas import tpu as pltpu

F32 = jnp.float32
BF16 = jnp.bfloat16
I32 = jnp.int32

D_MODEL = 2048
RMS_EPS = 1e-5
ROPE_THETA = 10000.0
RET_HEADS = 8
RET_DK = 128
RET_DV = 256
RET_CHUNK = 128
SWA_Q_HEADS = 32
SWA_KV_HEADS = 4
SWA_HEAD_DIM = 64
SWA_WINDOW = 128
SWA_BLOCK = 128
N_EXPERTS = 32
TOP_K = 4
D_FF = 2048
SWIGLU_LIMIT = 7.0
SWIGLU_ALPHA = 1.702

RET_QK_W = RET_HEADS * RET_DK
RET_V_W = RET_HEADS * RET_DV
SWA_Q_W = SWA_Q_HEADS * SWA_HEAD_DIM
SWA_KV_W = SWA_KV_HEADS * SWA_HEAD_DIM
GATE_W = 2 * D_MODEL
IN_WIDTH = 2 * RET_QK_W + 2 * RET_V_W + SWA_Q_W + 2 * SWA_KV_W + GATE_W

LANES = 128
VMEM_LIMIT = 56 * 1024 * 1024
NEG_BIG = -1e30

_NT = (((1,), (1,)), ((), ()))
_TN = (((0,), (0,)), ((), ()))


def _cparams(sem):
    return pltpu.CompilerParams(dimension_semantics=sem, vmem_limit_bytes=VMEM_LIMIT)


def _sigmoid(x):
    return 1.0 / (1.0 + jnp.exp(-x))


def _rope_table_kernel(pos_ref, invr_ref, invs_ref, cr_ref, sr_ref, cs_ref, sa_ref, sb_ref):
    pos = pos_ref[...]
    lane = lax.broadcasted_iota(I32, (pos.shape[0], LANES), 1)
    ang_r = pos * invr_ref[...]
    cr_ref[...] = jnp.cos(ang_r)
    sr_ref[...] = jnp.where(lane < RET_DK // 2, -1.0, 1.0) * jnp.sin(ang_r)
    ang_s = pos * invs_ref[...]
    sin_s = jnp.sin(ang_s)
    first_half = (lane % SWA_HEAD_DIM) < SWA_HEAD_DIM // 2
    cs_ref[...] = jnp.cos(ang_s)
    sa_ref[...] = jnp.where(first_half, -sin_s, 0.0)
    sb_ref[...] = jnp.where(first_half, 0.0, sin_s)


def _rope_tables(positions):
    n = positions.size
    pos = positions.reshape(n, 1).astype(F32)
    inv_r = 1.0 / (ROPE_THETA ** (jnp.arange(0, RET_DK, 2, dtype=F32) / RET_DK))
    inv_s = 1.0 / (ROPE_THETA ** (jnp.arange(0, SWA_HEAD_DIM, 2, dtype=F32) / SWA_HEAD_DIM))
    inv_r = jnp.tile(inv_r, LANES // inv_r.size).reshape(1, LANES)
    inv_s = jnp.tile(inv_s, LANES // inv_s.size).reshape(1, LANES)
    t = min(2048, n)
    tab = pl.BlockSpec((t, LANES), lambda i: (i, 0))
    row = pl.BlockSpec((1, LANES), lambda i: (0, 0))
    return pl.pallas_call(
        _rope_table_kernel,
        grid=(n // t,),
        in_specs=[pl.BlockSpec((t, 1), lambda i: (i, 0)), row, row],
        out_specs=[tab] * 5,
        out_shape=[jax.ShapeDtypeStruct((n, LANES), F32)] * 5,
        compiler_params=_cparams(("parallel",)),
        name="rope_tables",
    )(pos, inv_r, inv_s)


def _inproj_kernel(x_ref, nw_ref, w_ref, o_ref, xn_ref):
    @pl.when(pl.program_id(1) == 0)
    def _():
        x = x_ref[...]
        ms = jnp.mean(x * x, axis=-1, keepdims=True)
        xn_ref[...] = (x * lax.rsqrt(ms + RMS_EPS) * nw_ref[...]).astype(BF16)

    o_ref[...] = jnp.dot(xn_ref[...], w_ref[...], preferred_element_type=F32).astype(o_ref.dtype)


def _in_projection(x2, norm_w, w_in_bf16):
    n, d = x2.shape
    width = w_in_bf16.shape[1]
    tm = min(1024, n)
    tn = 1280
    return pl.pallas_call(
        _inproj_kernel,
        grid=(n // tm, width // tn),
        in_specs=[pl.BlockSpec((tm, d), lambda i, j: (i, 0)),
                  pl.BlockSpec((1, d), lambda i, j: (0, 0)),
                  pl.BlockSpec((d, tn), lambda i, j: (0, j))],
        out_specs=pl.BlockSpec((tm, tn), lambda i, j: (i, j)),
        out_shape=jax.ShapeDtypeStruct((n, width), BF16),
        scratch_shapes=[pltpu.VMEM((tm, d), BF16)],
        compiler_params=_cparams(("parallel", "arbitrary")),
        name="in_projection",
    )(x2, norm_w.reshape(1, d), w_in_bf16)


def _retention_kernel(q_ref, k_ref, v_ref, g_ref, cr_ref, sr_ref, o_ref, st_ref):
    @pl.when(pl.program_id(1) == 0)
    def _():
        st_ref[...] = jnp.zeros_like(st_ref)

    c = RET_CHUNK
    cr = cr_ref[...]
    sr = sr_ref[...]
    row = lax.broadcasted_iota(I32, (c, c), 0)
    col = lax.broadcasted_iota(I32, (c, c), 1)
    rel = (row - col).astype(F32)
    n_row = row.astype(F32)
    scale = RET_DK ** -0.5
    for h in range(RET_HEADS):
        log_gamma = math.log1p(-(2.0 ** (-5.0 - h)))
        q = q_ref[:, h * RET_DK:(h + 1) * RET_DK].astype(F32)
        k = k_ref[:, h * RET_DK:(h + 1) * RET_DK].astype(F32)
        qr = q * cr + pltpu.roll(q, RET_DK // 2, 1) * sr
        kr = k * cr + pltpu.roll(k, RET_DK // 2, 1) * sr
        decay = jnp.where(rel >= 0, jnp.exp(log_gamma * jnp.maximum(rel, 0.0)), 0.0) * scale
        s = lax.dot_general(qr.astype(BF16), kr.astype(BF16), _NT, preferred_element_type=F32) * decay
        v = v_ref[:, h * RET_DV:(h + 1) * RET_DV]
        inner = jnp.dot(s.astype(BF16), v, preferred_element_type=F32)
        xi = jnp.exp(log_gamma * (n_row + 1.0))
        st = st_ref[h]
        cross = jnp.dot((qr * xi).astype(BF16), st.astype(BF16), preferred_element_type=F32)
        o = inner + cross
        zeta = jnp.exp(log_gamma * (c - 1.0 - n_row)) * scale
        u = lax.dot_general((kr * zeta).astype(BF16), v, _TN, preferred_element_type=F32)
        st_ref[h] = math.exp(log_gamma * c) * st + u
        o = o * lax.rsqrt(jnp.mean(o * o, axis=-1, keepdims=True) + RMS_EPS)
        g = g_ref[:, h * RET_DV:(h + 1) * RET_DV].astype(F32)
        o_ref[:, h * RET_DV:(h + 1) * RET_DV] = (g * _sigmoid(g) * o).astype(o_ref.dtype)


def _retention(proj3, cr3, sr3):
    b, s, _ = proj3.shape
    c = RET_CHUNK
    return pl.pallas_call(
        _retention_kernel,
        grid=(b, s // c),
        in_specs=[pl.BlockSpec((None, c, RET_QK_W), lambda i, j: (i, j, 0)),
                  pl.BlockSpec((None, c, RET_QK_W), lambda i, j: (i, j, 1)),
                  pl.BlockSpec((None, c, RET_V_W), lambda i, j: (i, j, 1)),
                  pl.BlockSpec((None, c, RET_V_W), lambda i, j: (i, j, 2)),
                  pl.BlockSpec((None, c, LANES), lambda i, j: (i, j, 0)),
                  pl.BlockSpec((None, c, LANES), lambda i, j: (i, j, 0))],
        out_specs=pl.BlockSpec((None, c, RET_V_W), lambda i, j: (i, j, 0)),
        out_shape=jax.ShapeDtypeStruct((b, s, RET_V_W), BF16),
        scratch_shapes=[pltpu.VMEM((RET_HEADS, RET_DK, RET_DV), F32)],
        compiler_params=_cparams(("arbitrary", "arbitrary")),
        name="retention",
    )(proj3, proj3, proj3, proj3, cr3, sr3)


def _swa_kernel(sinks_ref, q_ref, kv_ref, kvp_ref, cs_ref, sa_ref, sb_ref, csp_ref, sap_ref, sbp_ref, o_ref):
    blk = pl.program_id(1)
    wb = SWA_BLOCK
    hd = SWA_HEAD_DIM
    group = SWA_Q_HEADS // SWA_KV_HEADS

    def rope(x, cs, sa, sb):
        return x * cs + pltpu.roll(x, LANES - hd // 2, 1) * sa + pltpu.roll(x, hd // 2, 1) * sb

    def rope_keys(ref, tabs):
        return [rope(ref[:, j * LANES:(j + 1) * LANES].astype(F32), *tabs) for j in range(SWA_KV_W // LANES)]

    cur_tabs = (cs_ref[...], sa_ref[...], sb_ref[...])
    kc = rope_keys(kv_ref, cur_tabs)
    kp = rope_keys(kvp_ref, (csp_ref[...], sap_ref[...], sbp_ref[...]))
    vc = kv_ref[:, SWA_KV_W:2 * SWA_KV_W].astype(F32)
    vp = kvp_ref[:, SWA_KV_W:2 * SWA_KV_W].astype(F32)

    qr = [rope(q_ref[:, j * LANES:(j + 1) * LANES].astype(F32), *cur_tabs) * (hd ** -0.5)
          for j in range(SWA_Q_W // LANES)]

    def head_cols(slabs, h):
        return slabs[h // 2][:, (h % 2) * hd:(h % 2 + 1) * hd]

    cols = group * wb
    kj = lax.broadcasted_iota(I32, (2 * wb, cols), 0)
    qi = lax.broadcasted_iota(I32, (2 * wb, cols), 1) & (wb - 1)
    valid = (kj > qi) & (kj <= qi + SWA_WINDOW) & ((kj >= wb) | (blk > 0))

    for kh in range(SWA_KV_HEADS):
        kband = jnp.concatenate([head_cols(kp, kh), head_cols(kc, kh)], axis=0).astype(BF16)
        vband = jnp.concatenate([vp[:, kh * hd:(kh + 1) * hd], vc[:, kh * hd:(kh + 1) * hd]], axis=0).astype(BF16)
        heads = [kh * group + g for g in range(group)]
        qg = jnp.concatenate([head_cols(qr, h) for h in heads], axis=0).astype(BF16)
        sink = jnp.concatenate([jnp.full((1, wb), sinks_ref[h], F32) for h in heads], axis=1)
        s = lax.dot_general(kband, qg, _NT, preferred_element_type=F32)
        s = jnp.where(valid, s, NEG_BIG)
        m = jnp.maximum(jnp.max(s, axis=0, keepdims=True), sink)
        p = jnp.exp(s - m)
        denom = jnp.sum(p, axis=0, keepdims=True) + jnp.exp(sink - m)
        o = lax.dot_general((p * (1.0 / denom)).astype(BF16), vband, _TN, preferred_element_type=F32)
        for g in range(0, group, 2):
            pair = jnp.concatenate([o[g * wb:(g + 1) * wb], o[(g + 1) * wb:(g + 2) * wb]], axis=1)
            h = heads[g]
            o_ref[:, h * hd:(h + 2) * hd] = pair.astype(o_ref.dtype)


def _sliding_window(proj3, sinks, cs3, sa3, sb3):
    b, s, _ = proj3.shape
    wb = SWA_BLOCK
    q_blk = (2 * RET_QK_W + 2 * RET_V_W) // SWA_Q_W
    kv_blk = (2 * RET_QK_W + 2 * RET_V_W + SWA_Q_W + GATE_W) // (2 * SWA_KV_W)
    prev = lambda j: jnp.maximum(j - 1, 0)
    tab = pl.BlockSpec((None, wb, LANES), lambda i, j: (i, j, 0))
    tab_prev = pl.BlockSpec((None, wb, LANES), lambda i, j: (i, prev(j), 0))
    return pl.pallas_call(
        _swa_kernel,
        grid=(b, s // wb),
        in_specs=[pl.BlockSpec(memory_space=pltpu.SMEM),
                  pl.BlockSpec((None, wb, SWA_Q_W), lambda i, j: (i, j, q_blk)),
                  pl.BlockSpec((None, wb, 2 * SWA_KV_W), lambda i, j: (i, j, kv_blk)),
                  pl.BlockSpec((None, wb, 2 * SWA_KV_W), lambda i, j: (i, prev(j), kv_blk)),
                  tab, tab, tab, tab_prev, tab_prev, tab_prev],
        out_specs=pl.BlockSpec((None, wb, SWA_Q_W), lambda i, j: (i, j, 0)),
        out_shape=jax.ShapeDtypeStruct((b, s, SWA_Q_W), BF16),
        compiler_params=_cparams(("parallel", "parallel")),
        name="sliding_window",
    )(sinks, proj3, proj3, proj3, cs3, sa3, sb3, cs3, sa3, sb3)


def _post_kernel(ret_ref, swa_ref, gl_ref, x_ref, gb_ref, wro_ref, wso_ref, wo_ref, fw_ref, rwt_ref, rb_ref,
                 h1_ref, xn2_ref, idx_ref, tw_ref, rank_ref, cnt_ref, carry_ref):
    @pl.when(pl.program_id(0) == 0)
    def _():
        carry_ref[...] = jnp.zeros_like(carry_ref)

    d = D_MODEL
    rows = x_ref.shape[0]
    n_sub = 2
    for r in range(n_sub):
        rs = slice(r * rows // n_sub, (r + 1) * rows // n_sub)
        a = jnp.dot(ret_ref[rs, :], wro_ref[...], preferred_element_type=F32)
        b = jnp.dot(swa_ref[rs, :], wso_ref[...], preferred_element_type=F32)
        gl = gl_ref[rs, :].astype(F32) + gb_ref[...]
        mix = _sigmoid(gl[:, :d]) * a + _sigmoid(gl[:, d:]) * b
        h1 = x_ref[rs, :] + jnp.dot(mix.astype(BF16), wo_ref[...], preferred_element_type=F32)
        h1_ref[rs, :] = h1
        xn2_ref[rs, :] = h1 * lax.rsqrt(jnp.mean(h1 * h1, axis=-1, keepdims=True) + RMS_EPS) * fw_ref[...]
    xn2 = xn2_ref[...]

    xh = xn2.astype(BF16)
    xl = (xn2 - xh.astype(F32)).astype(BF16)
    rw = rwt_ref[...]
    rh = rw.astype(BF16)
    rl = (rw - rh.astype(F32)).astype(BF16)
    logits = (lax.dot_general(rh, xh, _NT, preferred_element_type=F32)
              + lax.dot_general(rh, xl, _NT, preferred_element_type=F32)
              + lax.dot_general(rl, xh, _NT, preferred_element_type=F32)) + rb_ref[...]
    tm = logits.shape[1]
    e_iota = lax.broadcasted_iota(I32, (N_EXPERTS, tm), 0)
    vals, idxs = [], []
    for _ in range(TOP_K):
        m = jnp.max(logits, axis=0, keepdims=True)
        ix = jnp.min(jnp.where(logits == m, e_iota, N_EXPERTS), axis=0, keepdims=True)
        vals.append(m)
        idxs.append(ix)
        logits = jnp.where(e_iota == ix, -jnp.inf, logits)
    ex = [jnp.exp(v - vals[0]) for v in vals]
    den = ex[0] + ex[1] + ex[2] + ex[3]
    tw_ref[...] = jnp.concatenate([e / den for e in ex], axis=0)
    idx_ref[...] = jnp.concatenate(idxs, axis=0)

    onehot = jnp.zeros((N_EXPERTS, tm), F32)
    for ix in idxs:
        onehot = onehot + (e_iota == ix).astype(F32)
    earlier = (lax.broadcasted_iota(I32, (tm, tm), 0) < lax.broadcasted_iota(I32, (tm, tm), 1)).astype(BF16)
    prefix = jnp.dot(onehot.astype(BF16), earlier, preferred_element_type=F32) + carry_ref[:, 0:1]
    ranks = [jnp.sum(jnp.where(e_iota == ix, prefix, 0.0), axis=0, keepdims=True) for ix in idxs]
    rank_ref[...] = jnp.concatenate(ranks, axis=0).astype(I32)
    carry = carry_ref[...] + jnp.sum(onehot, axis=1, keepdims=True)
    carry_ref[...] = carry
    cnt_ref[...] = carry


def _post_attention(ret2, swa2, proj2, x2, gate_bias, wro, wso, wo, ffn_w, router_w, router_b):
    n, d = x2.shape
    tm = min(256, n)
    gl_blk = (2 * RET_QK_W + 2 * RET_V_W + SWA_Q_W) // GATE_W
    const = lambda shape: pl.BlockSpec(shape, lambda i: (0,) * len(shape), pipeline_mode=pl.Buffered(1))
    rowblk = lambda w: pl.BlockSpec((tm, w), lambda i: (i, 0))
    tokrow = pl.BlockSpec((TOP_K, tm), lambda i: (0, i))
    return pl.pallas_call(
        _post_kernel,
        grid=(n // tm,),
        in_specs=[rowblk(RET_V_W), rowblk(SWA_Q_W),
                  pl.BlockSpec((tm, GATE_W), lambda i: (i, gl_blk)),
                  rowblk(d), const((1, GATE_W)),
                  const((RET_V_W, d)), const((SWA_Q_W, d)), const((d, d)),
                  const((1, d)), const((N_EXPERTS, d)), const((N_EXPERTS, 1))],
        out_specs=[rowblk(d), rowblk(d), tokrow, tokrow, tokrow,
                   pl.BlockSpec((N_EXPERTS, LANES), lambda i: (0, 0))],
        out_shape=[jax.ShapeDtypeStruct((n, d), F32), jax.ShapeDtypeStruct((n, d), F32),
                   jax.ShapeDtypeStruct((TOP_K, n), I32), jax.ShapeDtypeStruct((TOP_K, n), F32),
                   jax.ShapeDtypeStruct((TOP_K, n), I32), jax.ShapeDtypeStruct((N_EXPERTS, LANES), F32)],
        scratch_shapes=[pltpu.VMEM((N_EXPERTS, LANES), F32)],
        compiler_params=_cparams(("arbitrary",)),
        name="post_attention",
    )(ret2, swa2, proj2, x2, gate_bias.reshape(1, GATE_W), wro, wso, wo, ffn_w.reshape(1, d),
      router_w.T, router_b.reshape(N_EXPERTS, 1))


def _dest_kernel(idx_ref, rank_ref, gs_ref, o_ref):
    t = idx_ref.shape[1]
    e_iota = lax.broadcasted_iota(I32, (N_EXPERTS, t), 0)
    gs = gs_ref[...]
    rows = []
    for k in range(TOP_K):
        start = jnp.sum(jnp.where(e_iota == idx_ref[k:k + 1, :], gs, 0), axis=0, keepdims=True)
        rows.append(start + rank_ref[k:k + 1, :])
    o_ref[...] = jnp.concatenate(rows, axis=0)


def _dest_rows(top_idx, rank, group_start):
    n = top_idx.shape[1]
    t = min(2048, n)
    blk = pl.BlockSpec((TOP_K, t), lambda i: (0, i))
    return pl.pallas_call(
        _dest_kernel,
        grid=(n // t,),
        in_specs=[blk, blk, pl.BlockSpec((N_EXPERTS, 1), lambda i: (0, 0))],
        out_specs=blk,
        out_shape=jax.ShapeDtypeStruct((TOP_K, n), I32),
        compiler_params=_cparams(("parallel",)),
        name="dest_rows",
    )(top_idx, rank, group_start.reshape(N_EXPERTS, 1))


def _row_copy(src, s, dst, d, sem):
    return pltpu.make_async_copy(src.at[pl.ds(s, 1), :], dst.at[pl.ds(d, 1), :], sem)


def _dispatch_kernel(dest_ref, pad_lo_ref, pad_n_ref, na_ref, x_ref, o_ref, zero_ref, sem, zsem, *, tm):
    i = pl.program_id(0)
    tq = x_ref.shape[0]
    n = pl.num_programs(0) * tq
    n_tiles = o_ref.shape[0] // tm

    @pl.when(i == 0)
    def _():
        zero_ref[...] = jnp.zeros_like(zero_ref)

        def pad_rows(e, total):
            def one(r, carry):
                _row_copy(zero_ref, 0, o_ref, pad_lo_ref[e] + r, zsem).start()
                return carry
            lax.fori_loop(0, pad_n_ref[e], one, 0)
            return total + pad_n_ref[e]

        n_pad = lax.fori_loop(0, N_EXPERTS, pad_rows, 0)

        def tile_copy(t):
            return pltpu.make_async_copy(zero_ref, o_ref.at[pl.ds(t * tm, tm), :], zsem)

        def start_tile(t, carry):
            tile_copy(t).start()
            return carry

        def drain_tile(t, carry):
            tile_copy(0).wait()
            return carry

        def drain_row(r, carry):
            _row_copy(zero_ref, 0, o_ref, 0, zsem).wait()
            return carry

        lax.fori_loop(na_ref[0], n_tiles, start_tile, 0)
        lax.fori_loop(0, n_pad, drain_row, 0)
        lax.fori_loop(na_ref[0], n_tiles, drain_tile, 0)

    def issue(r, carry):
        for k in range(TOP_K):
            _row_copy(x_ref, r, o_ref, dest_ref[k * n + i * tq + r], sem).start()
        return carry

    lax.fori_loop(0, tq, issue, 0, unroll=8)
    for _ in range(TOP_K):
        pltpu.make_async_copy(x_ref, o_ref.at[pl.ds(0, tq), :], sem).wait()


def _dispatch(dest_flat, pad_lo, pad_n, n_active, xn2, n_rows, tm):
    n, d = xn2.shape
    tq = min(512, n)
    return pl.pallas_call(
        functools.partial(_dispatch_kernel, tm=tm),
        grid_spec=pltpu.PrefetchScalarGridSpec(
            num_scalar_prefetch=4,
            grid=(n // tq,),
            in_specs=[pl.BlockSpec((tq, d), lambda i, *_: (i, 0))],
            out_specs=pl.BlockSpec(memory_space=pl.ANY),
            scratch_shapes=[pltpu.VMEM((tm, d), F32), pltpu.SemaphoreType.DMA(()), pltpu.SemaphoreType.DMA(())],
        ),
        out_shape=jax.ShapeDtypeStruct((n_rows, d), F32),
        compiler_params=_cparams(("arbitrary",)),
        name="dispatch_rows",
    )(dest_flat, pad_lo, pad_n, n_active, xn2)


GU_CHUNK = 512
DN_CHUNK = 256


IDLE_STEP = -1
FILL_STEP = -2


def _work_schedule(tiles_per_expert, n_chunks, n_steps, n_tiles):
    e_ids = jnp.arange(N_EXPERTS, dtype=I32)
    steps_e = jnp.where(e_ids < N_EXPERTS - 1, jnp.maximum(tiles_per_expert, n_chunks), tiles_per_expert)
    end_e = n_chunks + jnp.cumsum(steps_e)
    start_e = end_e - steps_e
    first_tile_e = jnp.cumsum(tiles_per_expert) - tiles_per_expert
    w = jnp.arange(n_steps, dtype=I32)[:, None]
    prologue = w[:, 0] < n_chunks
    in_e = jnp.logical_and(w >= start_e[None, :], w < end_e[None, :])
    local_e = w - start_e[None, :]
    tile_e = jnp.logical_and(in_e, local_e < tiles_per_expert[None, :])
    conv_e = jnp.logical_and(jnp.logical_and(in_e, local_e < n_chunks), e_ids[None, :] < N_EXPERTS - 1)
    has_tile = jnp.any(tile_e, axis=1)
    has_conv = jnp.logical_or(prologue, jnp.any(conv_e, axis=1))
    tile_id = jnp.sum(jnp.where(tile_e, first_tile_e[None, :] + local_e, 0), axis=1)
    cur_expert = jnp.sum(jnp.where(in_e, e_ids[None, :], 0), axis=1)
    cur_expert = jnp.where(w[:, 0] >= end_e[N_EXPERTS - 1], N_EXPERTS - 1, cur_expert)
    n_used = jnp.sum(tiles_per_expert)
    spare = w[:, 0] - end_e[N_EXPERTS - 1]
    fill = jnp.logical_and(spare >= 0, n_used + spare < n_tiles)
    tiles_done = jnp.sum(jnp.clip(w + 1 - start_e[None, :], 0, tiles_per_expert[None, :]), axis=1)
    fills_done = jnp.clip(spare + 1, 0, n_tiles - n_used)
    convs_done = jnp.clip(w[:, 0] + 1, 0, n_chunks) + jnp.sum(
        jnp.where(e_ids[None, :] < N_EXPERTS - 1, jnp.clip(w + 1 - start_e[None, :], 0, n_chunks), 0), axis=1)
    in_blk = jnp.maximum(tiles_done - 1, 0)
    out_blk = jnp.maximum(tiles_done + fills_done - 1, 0)
    conv_lin = jnp.maximum(convs_done - 1, 0)
    tile = jnp.where(has_tile, tile_id, jnp.where(fill, FILL_STEP, IDLE_STEP))
    return (tile.astype(I32), in_blk.astype(I32), out_blk.astype(I32), has_conv.astype(I32),
            (conv_lin // n_chunks).astype(I32), (conv_lin % n_chunks).astype(I32), (cur_expert % 2).astype(I32),
            cur_expert.astype(I32))


def _gate_up_kernel(tile_ref, iblk_ref, oblk_ref, conv_ref, cexp_ref, cchunk_ref, slot_ref, bexp_ref,
                    x_ref, w32_ref, bg_ref, bl_ref, o_ref, wbuf_ref, wscale_ref):
    del iblk_ref, oblk_ref, bexp_ref
    step = pl.program_id(0)

    @pl.when(tile_ref[step] == FILL_STEP)
    def _():
        o_ref[...] = jnp.zeros_like(o_ref)

    half = GU_CHUNK // 2
    n_slabs = 2 * D_FF // GU_CHUNK

    @pl.when(conv_ref[step] == 1)
    def _():
        src = lax.broadcasted_iota(I32, (half, half), 0)
        dst = lax.broadcasted_iota(I32, (half, half), 1)
        perm = (src == jnp.where(dst < half // 2, 2 * dst, 2 * (dst - half // 2) + 1)).astype(BF16)
        parts = [jnp.dot(w32_ref[:, j * half:(j + 1) * half].astype(BF16), perm,
                         preferred_element_type=F32) for j in range(2)]
        cslot = cexp_ref[step] % 2
        c = cchunk_ref[step]
        for slab, lo in ((c, 0), (n_slabs + c, half // 2)):
            w8, scale = _quantize_columns(jnp.concatenate([p[:, lo:lo + half // 2] for p in parts], axis=1))
            wbuf_ref[cslot, slab] = w8
            wscale_ref[cslot, slab] = scale

    @pl.when(tile_ref[step] >= 0)
    def _():
        slot = slot_ref[step]
        x, x_scale = _quantize_rows(x_ref[...])
        per = 2
        for c in range(0, n_slabs, per):
            sl = slice(c * half, (c + per) * half)
            wg = jnp.concatenate([wbuf_ref[slot, c + j] for j in range(per)], axis=1)
            wl = jnp.concatenate([wbuf_ref[slot, n_slabs + c + j] for j in range(per)], axis=1)
            sg = jnp.concatenate([wscale_ref[slot, c + j] for j in range(per)], axis=1)
            sl_scale = jnp.concatenate([wscale_ref[slot, n_slabs + c + j] for j in range(per)], axis=1)
            gate = jnp.dot(x, wg, preferred_element_type=F32) * (x_scale * sg) + bg_ref[:, sl]
            lin = jnp.dot(x, wl, preferred_element_type=F32) * (x_scale * sl_scale) + bl_ref[:, sl]
            gate = jnp.minimum(gate, SWIGLU_LIMIT)
            lin = jnp.clip(lin, -SWIGLU_LIMIT, SWIGLU_LIMIT)
            o_ref[:, sl] = (gate * _sigmoid(SWIGLU_ALPHA * gate) * (lin + 1.0)).astype(o_ref.dtype)


F8 = jnp.float8_e4m3fn
F8_MAX = 448.0
TINY = 1e-30


def _quantize_columns(w):
    amax = jnp.maximum(jnp.max(jnp.abs(w), axis=0, keepdims=True), TINY)
    return (w * (F8_MAX / amax)).astype(F8), amax * (1.0 / F8_MAX)


def _quantize_rows(x):
    amax = jnp.maximum(jnp.max(jnp.abs(x), axis=1, keepdims=True), TINY)
    return (x * (F8_MAX / amax)).astype(F8), amax * (1.0 / F8_MAX)


def _down_kernel(tile_ref, iblk_ref, oblk_ref, conv_ref, cexp_ref, cchunk_ref, slot_ref, bexp_ref,
                 a_ref, w32_ref, bd_ref, o_ref, wbuf_ref, wscale_ref):
    del iblk_ref, oblk_ref, bexp_ref
    step = pl.program_id(0)

    @pl.when(tile_ref[step] == FILL_STEP)
    def _():
        o_ref[...] = jnp.zeros_like(o_ref)


    @pl.when(conv_ref[step] == 1)
    def _():
        w8, scale = _quantize_columns(w32_ref[...])
        cslot = cexp_ref[step] % 2
        wbuf_ref[cslot, cchunk_ref[step]] = w8
        wscale_ref[cslot, cchunk_ref[step]] = scale

    @pl.when(tile_ref[step] >= 0)
    def _():
        slot = slot_ref[step]
        a8, a_scale = _quantize_rows(a_ref[...].astype(F32))
        for c in range(D_MODEL // DN_CHUNK):
            sl = slice(c * DN_CHUNK, (c + 1) * DN_CHUNK)
            y = jnp.dot(a8, wbuf_ref[slot, c], preferred_element_type=F32)
            o_ref[:, sl] = y * (a_scale * wscale_ref[slot, c]) + bd_ref[:, sl]


def _expert_ffn(tiles_per_expert, xs, wgu, bg, bl, wd, bd, tm):
    n_rows, d = xs.shape
    n_tiles = n_rows // tm
    f = wd.shape[1]
    rows_in = lambda w: pl.BlockSpec((tm, w), lambda s, tile, iblk, *_: (iblk[s], 0))
    rows_out = lambda w: pl.BlockSpec((tm, w), lambda s, tile, iblk, oblk, *_: (oblk[s], 0))
    bias = lambda w: pl.BlockSpec((None, 1, w), lambda s, *p: (p[7][s], 0, 0))

    n_chunks = 2 * f // GU_CHUNK
    assert n_chunks == d // DN_CHUNK
    n_steps = n_chunks + n_tiles + n_chunks * N_EXPERTS
    schedule = _work_schedule(tiles_per_expert, n_chunks, n_steps, n_tiles)
    act = pl.pallas_call(
        _gate_up_kernel,
        grid_spec=pltpu.PrefetchScalarGridSpec(
            num_scalar_prefetch=8,
            grid=(n_steps,),
            in_specs=[rows_in(d),
                      pl.BlockSpec((None, d, GU_CHUNK),
                                   lambda s, t, i, o, c, cexp, cchunk, *_: (cexp[s], 0, cchunk[s])),
                      bias(f), bias(f)],
            out_specs=rows_out(f),
            scratch_shapes=[pltpu.VMEM((2, 2 * n_chunks, d, GU_CHUNK // 2), F8),
                            pltpu.VMEM((2, 2 * n_chunks, 1, GU_CHUNK // 2), F32)],
        ),
        out_shape=jax.ShapeDtypeStruct((n_rows, f), BF16),
        compiler_params=_cparams(("arbitrary",)),
        name="expert_gate_up",
    )(*schedule, xs, wgu, bg, bl)

    return pl.pallas_call(
        _down_kernel,
        grid_spec=pltpu.PrefetchScalarGridSpec(
            num_scalar_prefetch=8,
            grid=(n_steps,),
            in_specs=[rows_in(f),
                      pl.BlockSpec((None, f, DN_CHUNK),
                                   lambda s, t, i, o, c, cexp, cchunk, *_: (cexp[s], 0, cchunk[s])),
                      bias(d)],
            out_specs=rows_out(d),
            scratch_shapes=[pltpu.VMEM((2, n_chunks, f, DN_CHUNK), F8),
                            pltpu.VMEM((2, n_chunks, 1, DN_CHUNK), F32)],
        ),
        out_shape=jax.ShapeDtypeStruct((n_rows, d), F32),
        compiler_params=_cparams(("arbitrary",)),
        name="expert_down",
    )(*schedule, act, wd, bd)


def _combine_kernel(dest_ref, y_ref, tw_ref, h1_ref, fw_ref, o_ref, buf_ref, sem, *, final_norm):
    i = pl.program_id(0)
    steps = pl.num_programs(0)
    tq = h1_ref.shape[0]
    n = steps * tq

    def issue(step, slot):
        def body(r, carry):
            for k in range(TOP_K):
                src = dest_ref[k * n + step * tq + r]
                pltpu.make_async_copy(y_ref.at[pl.ds(src, 1), :], buf_ref.at[slot, k, pl.ds(r, 1), :],
                                      sem.at[slot]).start()
            return carry
        lax.fori_loop(0, tq, body, 0, unroll=8)

    @pl.when(i == 0)
    def _():
        issue(0, 0)

    @pl.when(i + 1 < steps)
    def _():
        issue(i + 1, (i + 1) % 2)

    slot = i % 2
    for k in range(TOP_K):
        pltpu.make_async_copy(y_ref.at[pl.ds(0, tq), :], buf_ref.at[slot, k], sem.at[slot]).wait()

    acc = h1_ref[...]
    moe = tw_ref[:, 0:1] * buf_ref[slot, 0]
    for k in range(1, TOP_K):
        moe = moe + tw_ref[:, k:k + 1] * buf_ref[slot, k]
    acc = acc + moe
    if final_norm:
        acc = acc * lax.rsqrt(jnp.mean(acc * acc, axis=-1, keepdims=True) + RMS_EPS) * fw_ref[...]
    o_ref[...] = acc


def _combine(dest_flat, y, top_w_t, h1, final_w, final_norm):
    n, d = h1.shape
    tq = min(256, n)
    return pl.pallas_call(
        functools.partial(_combine_kernel, final_norm=final_norm),
        grid_spec=pltpu.PrefetchScalarGridSpec(
            num_scalar_prefetch=1,
            grid=(n // tq,),
            in_specs=[pl.BlockSpec(memory_space=pl.ANY),
                      pl.BlockSpec((tq, TOP_K), lambda i, dest: (i, 0)),
                      pl.BlockSpec((tq, d), lambda i, dest: (i, 0)),
                      pl.BlockSpec((1, d), lambda i, dest: (0, 0))],
            out_specs=pl.BlockSpec((tq, d), lambda i, dest: (i, 0)),
            scratch_shapes=[pltpu.VMEM((2, TOP_K, tq, d), F32), pltpu.SemaphoreType.DMA((2,))],
        ),
        out_shape=jax.ShapeDtypeStruct((n, d), F32),
        compiler_params=_cparams(("arbitrary",)),
        name="combine_rows",
    )(dest_flat, y, top_w_t, h1, final_w.reshape(1, d))


def _cast_kernel(w_ref, o_ref):
    o_ref[...] = w_ref[...].astype(o_ref.dtype)


def _cast_in_weights(w):
    d, width = w.shape
    tn = 2 * SWA_KV_W
    kv_blk = (2 * RET_QK_W + 2 * RET_V_W + SWA_Q_W) // tn
    n_blk = width // tn

    def source_block(j):
        return jnp.where(j < kv_blk, j, jnp.where(j < n_blk - 1, j + 1, kv_blk))

    return pl.pallas_call(
        _cast_kernel,
        grid=(n_blk,),
        in_specs=[pl.BlockSpec((d, tn), lambda j: (0, source_block(j)))],
        out_specs=pl.BlockSpec((d, tn), lambda j: (0, j)),
        out_shape=jax.ShapeDtypeStruct((d, width), BF16),
        compiler_params=_cparams(("parallel",)),
        name="cast_in_weights",
    )(w)


def _layer(h, tables, p, final_w, final_norm, expert_tile):
    b, s, d = h.shape
    n = b * s
    cr, sr, cs, sa, sb = tables
    x2 = h.reshape(n, d)
    proj = _in_projection(x2, p["attn_norm_w"], _cast_in_weights(p["w_in"]))
    proj3 = proj.reshape(b, s, IN_WIDTH)
    t3 = lambda t: t.reshape(b, s, LANES)
    ret = _retention(proj3, t3(cr), t3(sr))
    swa = _sliding_window(proj3, p["sinks"], t3(cs), t3(sa), t3(sb))
    h1, xn2, top_idx, top_w, rank, counts = _post_attention(
        ret.reshape(n, RET_V_W), swa.reshape(n, SWA_Q_W), proj, x2, p["gate_bias"],
        p["w_ret_out"].astype(BF16), p["w_swa_out"].astype(BF16), p["w_o"].astype(BF16),
        p["ffn_norm_w"], p["router_w"], p["router_b"])

    tm = expert_tile
    n_tiles = (n * TOP_K) // tm + N_EXPERTS
    cnt = counts[:, 0].astype(I32)
    padded = ((cnt + tm - 1) // tm) * tm
    ends = jnp.cumsum(padded)
    group_start = ends - padded
    n_active = (ends[-1] // tm).reshape(1)

    dest_flat = _dest_rows(top_idx, rank, group_start).reshape(TOP_K * n)
    xs = _dispatch(dest_flat, group_start + cnt, padded - cnt, n_active, xn2, n_tiles * tm, tm)

    bgu = p["b_gate_up"]
    y = _expert_ffn(padded // tm, xs, p["w_gate_up"], bgu[:, 0::2].reshape(N_EXPERTS, 1, D_FF),
                    bgu[:, 1::2].reshape(N_EXPERTS, 1, D_FF), p["w_down"], p["b_down"].reshape(N_EXPERTS, 1, d), tm)
    out = _combine(dest_flat, y, top_w.T, h1, final_w, final_norm)
    return out.reshape(b, s, d)


def kernel(x, positions, attn_norm_w, w_in, gate_bias, w_ret_out, w_swa_out, w_o, sinks, ffn_norm_w, router_w,
           router_b, w_gate_up, b_gate_up, w_down, b_down, final_norm_w):
    depth = w_in.shape[0]
    stacked = dict(attn_norm_w=attn_norm_w, w_in=w_in, gate_bias=gate_bias, w_ret_out=w_ret_out,
                   w_swa_out=w_swa_out, w_o=w_o, sinks=sinks, ffn_norm_w=ffn_norm_w, router_w=router_w,
                   router_b=router_b, w_gate_up=w_gate_up, b_gate_up=b_gate_up, w_down=w_down, b_down=b_down)
    tables = _rope_tables(positions)
    h = x
    for layer in range(depth):
        p = {name: w[layer] for name, w in stacked.items()}
        h = _layer(h, tables, p, final_norm_w, layer == depth - 1, expert_tile=256)
    return h
```

```python
import functools
import math

import jax
import jax.numpy as jnp
from jax import lax
from jax.experimental import pallas as pl
from jax.experimental.pallas import tpu as pltpu

F32 = jnp.float32
BF16 = jnp.bfloat16
I32 = jnp.int32

D_MODEL = 2048
RMS_EPS = 1e-5
ROPE_THETA = 10000.0
RET_HEADS = 8
RET_DK = 128
RET_DV = 256
RET_CHUNK = 128
SWA_Q_HEADS = 32
SWA_KV_HEADS = 4
SWA_HEAD_DIM = 64
SWA_WINDOW = 128
SWA_BLOCK = 128
N_EXPERTS = 32
TOP_K = 4
D_FF = 2048
SWIGLU_LIMIT = 7.0
SWIGLU_ALPHA = 1.702

RET_QK_W = RET_HEADS * RET_DK
RET_V_W = RET_HEADS * RET_DV
SWA_Q_W = SWA_Q_HEADS * SWA_HEAD_DIM
SWA_KV_W = SWA_KV_HEADS * SWA_HEAD_DIM
GATE_W = 2 * D_MODEL
IN_WIDTH = 2 * RET_QK_W + 2 * RET_V_W + SWA_Q_W + 2 * SWA_KV_W + GATE_W

LANES = 128
VMEM_LIMIT = 56 * 1024 * 1024
NEG_BIG = -1e30

_NT = (((1,), (1,)), ((), ()))
_TN = (((0,), (0,)), ((), ()))


def _cparams(sem):
    return pltpu.CompilerParams(dimension_semantics=sem, vmem_limit_bytes=VMEM_LIMIT)


def _sigmoid(x):
    return 1.0 / (1.0 + jnp.exp(-x))


def _rope_table_kernel(pos_ref, invr_ref, invs_ref, cr_ref, sr_ref, cs_ref, sa_ref, sb_ref):
    pos = pos_ref[...]
    lane = lax.broadcasted_iota(I32, (pos.shape[0], LANES), 1)
    ang_r = pos * invr_ref[...]
    cr_ref[...] = jnp.cos(ang_r)
    sr_ref[...] = jnp.where(lane < RET_DK // 2, -1.0, 1.0) * jnp.sin(ang_r)
    ang_s = pos * invs_ref[...]
    sin_s = jnp.sin(ang_s)
    first_half = (lane % SWA_HEAD_DIM) < SWA_HEAD_DIM // 2
    cs_ref[...] = jnp.cos(ang_s)
    sa_ref[...] = jnp.where(first_half, -sin_s, 0.0)
    sb_ref[...] = jnp.where(first_half, 0.0, sin_s)


def _rope_tables(positions):
    n = positions.size
    pos = positions.reshape(n, 1).astype(F32)
    inv_r = 1.0 / (ROPE_THETA ** (jnp.arange(0, RET_DK, 2, dtype=F32) / RET_DK))
    inv_s = 1.0 / (ROPE_THETA ** (jnp.arange(0, SWA_HEAD_DIM, 2, dtype=F32) / SWA_HEAD_DIM))
    inv_r = jnp.tile(inv_r, LANES // inv_r.size).reshape(1, LANES)
    inv_s = jnp.tile(inv_s, LANES // inv_s.size).reshape(1, LANES)
    t = min(2048, n)
    tab = pl.BlockSpec((t, LANES), lambda i: (i, 0))
    row = pl.BlockSpec((1, LANES), lambda i: (0, 0))
    return pl.pallas_call(
        _rope_table_kernel,
        grid=(n // t,),
        in_specs=[pl.BlockSpec((t, 1), lambda i: (i, 0)), row, row],
        out_specs=[tab] * 5,
        out_shape=[jax.ShapeDtypeStruct((n, LANES), F32)] * 5,
        compiler_params=_cparams(("parallel",)),
        name="rope_tables",
    )(pos, inv_r, inv_s)


def _inproj_kernel(x_ref, nw_ref, w_ref, o_ref, xn_ref):
    @pl.when(pl.program_id(1) == 0)
    def _():
        x = x_ref[...]
        ms = jnp.mean(x * x, axis=-1, keepdims=True)
        xn_ref[...] = (x * lax.rsqrt(ms + RMS_EPS) * nw_ref[...]).astype(BF16)

    o_ref[...] = jnp.dot(xn_ref[...], w_ref[...], preferred_element_type=F32).astype(o_ref.dtype)


def _in_projection(x2, norm_w, w_in_bf16):
    n, d = x2.shape
    width = w_in_bf16.shape[1]
    tm = min(1024, n)
    tn = 1280
    return pl.pallas_call(
        _inproj_kernel,
        grid=(n // tm, width // tn),
        in_specs=[pl.BlockSpec((tm, d), lambda i, j: (i, 0)),
                  pl.BlockSpec((1, d), lambda i, j: (0, 0)),
                  pl.BlockSpec((d, tn), lambda i, j: (0, j))],
        out_specs=pl.BlockSpec((tm, tn), lambda i, j: (i, j)),
        out_shape=jax.ShapeDtypeStruct((n, width), BF16),
        scratch_shapes=[pltpu.VMEM((tm, d), BF16)],
        compiler_params=_cparams(("parallel", "arbitrary")),
        name="in_projection",
    )(x2, norm_w.reshape(1, d), w_in_bf16)


def _retention_kernel(q_ref, k_ref, v_ref, g_ref, cr_ref, sr_ref, o_ref, st_ref, decay_ref, xi_ref, zeta_ref):
    c = RET_CHUNK

    @pl.when(jnp.logical_and(pl.program_id(0) == 0, pl.program_id(1) == 0))
    def _():
        row = lax.broadcasted_iota(I32, (c, c), 0)
        col = lax.broadcasted_iota(I32, (c, c), 1)
        rel = (row - col).astype(F32)
        n_row = row.astype(F32)
        scale = RET_DK ** -0.5
        for h in range(RET_HEADS):
            log_gamma = math.log1p(-(2.0 ** (-5.0 - h)))
            decay_ref[h] = jnp.where(rel >= 0, jnp.exp(log_gamma * jnp.maximum(rel, 0.0)), 0.0) * scale
            xi_ref[h] = jnp.exp(log_gamma * (n_row + 1.0))
            zeta_ref[h] = jnp.exp(log_gamma * (c - 1.0 - n_row)) * scale

    @pl.when(pl.program_id(1) == 0)
    def _():
        st_ref[...] = jnp.zeros_like(st_ref)

    cr = cr_ref[...]
    sr = sr_ref[...]
    for h in range(RET_HEADS):
        log_gamma = math.log1p(-(2.0 ** (-5.0 - h)))
        q = q_ref[:, h * RET_DK:(h + 1) * RET_DK].astype(F32)
        k = k_ref[:, h * RET_DK:(h + 1) * RET_DK].astype(F32)
        qr = q * cr + pltpu.roll(q, RET_DK // 2, 1) * sr
        kr = k * cr + pltpu.roll(k, RET_DK // 2, 1) * sr
        s = lax.dot_general(qr.astype(BF16), kr.astype(BF16), _NT, preferred_element_type=F32) * decay_ref[h]
        v = v_ref[:, h * RET_DV:(h + 1) * RET_DV]
        inner = jnp.dot(s.astype(BF16), v, preferred_element_type=F32)
        st = st_ref[h]
        cross = jnp.dot((qr * xi_ref[h]).astype(BF16), st.astype(BF16), preferred_element_type=F32)
        o = inner + cross
        u = lax.dot_general((kr * zeta_ref[h]).astype(BF16), v, _TN, preferred_element_type=F32)
        st_ref[h] = math.exp(log_gamma * c) * st + u
        o = o * lax.rsqrt(jnp.mean(o * o, axis=-1, keepdims=True) + RMS_EPS)
        g = g_ref[:, h * RET_DV:(h + 1) * RET_DV].astype(F32)
        o_ref[:, h * RET_DV:(h + 1) * RET_DV] = (g * _sigmoid(g) * o).astype(o_ref.dtype)


def _retention(proj3, cr3, sr3):
    b, s, _ = proj3.shape
    c = RET_CHUNK
    return pl.pallas_call(
        _retention_kernel,
        grid=(b, s // c),
        in_specs=[pl.BlockSpec((None, c, RET_QK_W), lambda i, j: (i, j, 0)),
                  pl.BlockSpec((None, c, RET_QK_W), lambda i, j: (i, j, 1)),
                  pl.BlockSpec((None, c, RET_V_W), lambda i, j: (i, j, 1)),
                  pl.BlockSpec((None, c, RET_V_W), lambda i, j: (i, j, 2)),
                  pl.BlockSpec((None, c, LANES), lambda i, j: (i, j, 0)),
                  pl.BlockSpec((None, c, LANES), lambda i, j: (i, j, 0))],
        out_specs=pl.BlockSpec((None, c, RET_V_W), lambda i, j: (i, j, 0)),
        out_shape=jax.ShapeDtypeStruct((b, s, RET_V_W), BF16),
        scratch_shapes=[pltpu.VMEM((RET_HEADS, RET_DK, RET_DV), F32)]
        + [pltpu.VMEM((RET_HEADS, c, c), F32)] * 3,
        compiler_params=_cparams(("arbitrary", "arbitrary")),
        name="retention",
    )(proj3, proj3, proj3, proj3, cr3, sr3)


def _swa_kernel(sinks_ref, q_ref, kv_ref, kvp_ref, cs_ref, sa_ref, sb_ref, csp_ref, sap_ref, sbp_ref, o_ref):
    blk = pl.program_id(1)
    wb = SWA_BLOCK
    hd = SWA_HEAD_DIM
    group = SWA_Q_HEADS // SWA_KV_HEADS

    def rope(x, cs, sa, sb):
        return x * cs + pltpu.roll(x, LANES - hd // 2, 1) * sa + pltpu.roll(x, hd // 2, 1) * sb

    def rope_keys(ref, tabs):
        return [rope(ref[:, j * LANES:(j + 1) * LANES].astype(F32), *tabs) for j in range(SWA_KV_W // LANES)]

    cur_tabs = (cs_ref[...], sa_ref[...], sb_ref[...])
    kc = rope_keys(kv_ref, cur_tabs)
    kp = rope_keys(kvp_ref, (csp_ref[...], sap_ref[...], sbp_ref[...]))
    vc = kv_ref[:, SWA_KV_W:2 * SWA_KV_W].astype(F32)
    vp = kvp_ref[:, SWA_KV_W:2 * SWA_KV_W].astype(F32)

    qr = [rope(q_ref[:, j * LANES:(j + 1) * LANES].astype(F32), *cur_tabs) * (hd ** -0.5)
          for j in range(SWA_Q_W // LANES)]

    def head_cols(slabs, h):
        return slabs[h // 2][:, (h % 2) * hd:(h % 2 + 1) * hd]

    cols = group * wb
    kj = lax.broadcasted_iota(I32, (2 * wb, cols), 0)
    qi = lax.broadcasted_iota(I32, (2 * wb, cols), 1) & (wb - 1)
    valid = (kj > qi) & (kj <= qi + SWA_WINDOW) & ((kj >= wb) | (blk > 0))

    for kh in range(SWA_KV_HEADS):
        kband = jnp.concatenate([head_cols(kp, kh), head_cols(kc, kh)], axis=0).astype(BF16)
        vband = jnp.concatenate([vp[:, kh * hd:(kh + 1) * hd], vc[:, kh * hd:(kh + 1) * hd]], axis=0).astype(BF16)
        heads = [kh * group + g for g in range(group)]
        qg = jnp.concatenate([head_cols(qr, h) for h in heads], axis=0).astype(BF16)
        sink = jnp.concatenate([jnp.full((1, wb), sinks_ref[h], F32) for h in heads], axis=1)
        s = lax.dot_general(kband, qg, _NT, preferred_element_type=F32)
        s = jnp.where(valid, s, NEG_BIG)
        m = jnp.maximum(jnp.max(s, axis=0, keepdims=True), sink)
        p = jnp.exp(s - m)
        denom = jnp.sum(p, axis=0, keepdims=True) + jnp.exp(sink - m)
        o = lax.dot_general((p * (1.0 / denom)).astype(BF16), vband, _TN, preferred_element_type=F32)
        for g in range(0, group, 2):
            pair = jnp.concatenate([o[g * wb:(g + 1) * wb], o[(g + 1) * wb:(g + 2) * wb]], axis=1)
            h = heads[g]
            o_ref[:, h * hd:(h + 2) * hd] = pair.astype(o_ref.dtype)


def _sliding_window(proj3, sinks, cs3, sa3, sb3):
    b, s, _ = proj3.shape
    wb = SWA_BLOCK
    q_blk = (2 * RET_QK_W + 2 * RET_V_W) // SWA_Q_W
    kv_blk = (2 * RET_QK_W + 2 * RET_V_W + SWA_Q_W + GATE_W) // (2 * SWA_KV_W)
    prev = lambda j: jnp.maximum(j - 1, 0)
    tab = pl.BlockSpec((None, wb, LANES), lambda i, j: (i, j, 0))
    tab_prev = pl.BlockSpec((None, wb, LANES), lambda i, j: (i, prev(j), 0))
    return pl.pallas_call(
        _swa_kernel,
        grid=(b, s // wb),
        in_specs=[pl.BlockSpec(memory_space=pltpu.SMEM),
                  pl.BlockSpec((None, wb, SWA_Q_W), lambda i, j: (i, j, q_blk)),
                  pl.BlockSpec((None, wb, 2 * SWA_KV_W), lambda i, j: (i, j, kv_blk)),
                  pl.BlockSpec((None, wb, 2 * SWA_KV_W), lambda i, j: (i, prev(j), kv_blk)),
                  tab, tab, tab, tab_prev, tab_prev, tab_prev],
        out_specs=pl.BlockSpec((None, wb, SWA_Q_W), lambda i, j: (i, j, 0)),
        out_shape=jax.ShapeDtypeStruct((b, s, SWA_Q_W), BF16),
        compiler_params=_cparams(("parallel", "parallel")),
        name="sliding_window",
    )(sinks, proj3, proj3, proj3, cs3, sa3, sb3, cs3, sa3, sb3)


def _post_kernel(ret_ref, swa_ref, gl_ref, x_ref, gb_ref, wro_ref, wso_ref, wo_ref, fw_ref, rwt_ref, rb_ref,
                 h1_ref, xn2_ref, idx_ref, tw_ref, rank_ref, cnt_ref, carry_ref):
    @pl.when(pl.program_id(0) == 0)
    def _():
        carry_ref[...] = jnp.zeros_like(carry_ref)

    d = D_MODEL
    rows = x_ref.shape[0]
    n_sub = 2
    for r in range(n_sub):
        rs = slice(r * rows // n_sub, (r + 1) * rows // n_sub)
        a = jnp.dot(ret_ref[rs, :], wro_ref[...], preferred_element_type=F32)
        b = jnp.dot(swa_ref[rs, :], wso_ref[...], preferred_element_type=F32)
        gl = gl_ref[rs, :].astype(F32) + gb_ref[...]
        mix = _sigmoid(gl[:, :d]) * a + _sigmoid(gl[:, d:]) * b
        h1 = x_ref[rs, :] + jnp.dot(mix.astype(BF16), wo_ref[...], preferred_element_type=F32)
        h1_ref[rs, :] = h1
        xn2_ref[rs, :] = h1 * lax.rsqrt(jnp.mean(h1 * h1, axis=-1, keepdims=True) + RMS_EPS) * fw_ref[...]
    xn2 = xn2_ref[...]

    xh = xn2.astype(BF16)
    xl = (xn2 - xh.astype(F32)).astype(BF16)
    rw = rwt_ref[...]
    rh = rw.astype(BF16)
    rl = (rw - rh.astype(F32)).astype(BF16)
    logits = (lax.dot_general(rh, xh, _NT, preferred_element_type=F32)
              + lax.dot_general(rh, xl, _NT, preferred_element_type=F32)
              + lax.dot_general(rl, xh, _NT, preferred_element_type=F32)) + rb_ref[...]
    tm = logits.shape[1]
    e_iota = lax.broadcasted_iota(I32, (N_EXPERTS, tm), 0)
    vals, idxs = [], []
    for _ in range(TOP_K):
        m = jnp.max(logits, axis=0, keepdims=True)
        ix = jnp.min(jnp.where(logits == m, e_iota, N_EXPERTS), axis=0, keepdims=True)
        vals.append(m)
        idxs.append(ix)
        logits = jnp.where(e_iota == ix, -jnp.inf, logits)
    ex = [jnp.exp(v - vals[0]) for v in vals]
    den = ex[0] + ex[1] + ex[2] + ex[3]
    tw_ref[...] = jnp.concatenate([e / den for e in ex], axis=0)
    idx_ref[...] = jnp.concatenate(idxs, axis=0)

    onehot = jnp.zeros((N_EXPERTS, tm), F32)
    for ix in idxs:
        onehot = onehot + (e_iota == ix).astype(F32)
    earlier = (lax.broadcasted_iota(I32, (tm, tm), 0) < lax.broadcasted_iota(I32, (tm, tm), 1)).astype(BF16)
    prefix = jnp.dot(onehot.astype(BF16), earlier, preferred_element_type=F32) + carry_ref[:, 0:1]
    ranks = [jnp.sum(jnp.where(e_iota == ix, prefix, 0.0), axis=0, keepdims=True) for ix in idxs]
    rank_ref[...] = jnp.concatenate(ranks, axis=0).astype(I32)
    carry = carry_ref[...] + jnp.sum(onehot, axis=1, keepdims=True)
    carry_ref[...] = carry
    cnt_ref[...] = carry


def _post_attention(ret2, swa2, proj2, x2, gate_bias, wro, wso, wo, ffn_w, router_w, router_b):
    n, d = x2.shape
    tm = min(256, n)
    gl_blk = (2 * RET_QK_W + 2 * RET_V_W + SWA_Q_W) // GATE_W
    const = lambda shape: pl.BlockSpec(shape, lambda i: (0,) * len(shape), pipeline_mode=pl.Buffered(1))
    rowblk = lambda w: pl.BlockSpec((tm, w), lambda i: (i, 0))
    tokrow = pl.BlockSpec((TOP_K, tm), lambda i: (0, i))
    return pl.pallas_call(
        _post_kernel,
        grid=(n // tm,),
        in_specs=[rowblk(RET_V_W), rowblk(SWA_Q_W),
                  pl.BlockSpec((tm, GATE_W), lambda i: (i, gl_blk)),
                  rowblk(d), const((1, GATE_W)),
                  const((RET_V_W, d)), const((SWA_Q_W, d)), const((d, d)),
                  const((1, d)), const((N_EXPERTS, d)), const((N_EXPERTS, 1))],
        out_specs=[rowblk(d), rowblk(d), tokrow, tokrow, tokrow,
                   pl.BlockSpec((N_EXPERTS, LANES), lambda i: (0, 0))],
        out_shape=[jax.ShapeDtypeStruct((n, d), F32), jax.ShapeDtypeStruct((n, d), F32),
                   jax.ShapeDtypeStruct((TOP_K, n), I32), jax.ShapeDtypeStruct((TOP_K, n), F32),
                   jax.ShapeDtypeStruct((TOP_K, n), I32), jax.ShapeDtypeStruct((N_EXPERTS, LANES), F32)],
        scratch_shapes=[pltpu.VMEM((N_EXPERTS, LANES), F32)],
        compiler_params=_cparams(("arbitrary",)),
        name="post_attention",
    )(ret2, swa2, proj2, x2, gate_bias.reshape(1, GATE_W), wro, wso, wo, ffn_w.reshape(1, d),
      router_w.T, router_b.reshape(N_EXPERTS, 1))


def _dest_kernel(idx_ref, rank_ref, gs_ref, o_ref):
    t = idx_ref.shape[1]
    e_iota = lax.broadcasted_iota(I32, (N_EXPERTS, t), 0)
    gs = gs_ref[...]
    rows = []
    for k in range(TOP_K):
        start = jnp.sum(jnp.where(e_iota == idx_ref[k:k + 1, :], gs, 0), axis=0, keepdims=True)
        rows.append(start + rank_ref[k:k + 1, :])
    o_ref[...] = jnp.concatenate(rows, axis=0)


def _dest_rows(top_idx, rank, group_start):
    n = top_idx.shape[1]
    t = min(2048, n)
    blk = pl.BlockSpec((TOP_K, t), lambda i: (0, i))
    return pl.pallas_call(
        _dest_kernel,
        grid=(n // t,),
        in_specs=[blk, blk, pl.BlockSpec((N_EXPERTS, 1), lambda i: (0, 0))],
        out_specs=blk,
        out_shape=jax.ShapeDtypeStruct((TOP_K, n), I32),
        compiler_params=_cparams(("parallel",)),
        name="dest_rows",
    )(top_idx, rank, group_start.reshape(N_EXPERTS, 1))


def _row_copy(src, s, dst, d, sem):
    return pltpu.make_async_copy(src.at[pl.ds(s, 1), :], dst.at[pl.ds(d, 1), :], sem)


def _dispatch_kernel(dest_ref, pad_lo_ref, pad_n_ref, na_ref, x_ref, o_ref, zero_ref, sem, zsem, *, tm):
    i = pl.program_id(0)
    tq = x_ref.shape[0]
    n = pl.num_programs(0) * tq
    n_tiles = o_ref.shape[0] // tm

    @pl.when(i == 0)
    def _():
        zero_ref[...] = jnp.zeros_like(zero_ref)

        def pad_rows(e, total):
            def one(r, carry):
                _row_copy(zero_ref, 0, o_ref, pad_lo_ref[e] + r, zsem).start()
                return carry
            lax.fori_loop(0, pad_n_ref[e], one, 0)
            return total + pad_n_ref[e]

        n_pad = lax.fori_loop(0, N_EXPERTS, pad_rows, 0)

        def tile_copy(t):
            return pltpu.make_async_copy(zero_ref, o_ref.at[pl.ds(t * tm, tm), :], zsem)

        def start_tile(t, carry):
            tile_copy(t).start()
            return carry

        def drain_tile(t, carry):
            tile_copy(0).wait()
            return carry

        def drain_row(r, carry):
            _row_copy(zero_ref, 0, o_ref, 0, zsem).wait()
            return carry

        lax.fori_loop(na_ref[0], n_tiles, start_tile, 0)
        lax.fori_loop(0, n_pad, drain_row, 0)
        lax.fori_loop(na_ref[0], n_tiles, drain_tile, 0)

    def issue(r, carry):
        for k in range(TOP_K):
            _row_copy(x_ref, r, o_ref, dest_ref[k * n + i * tq + r], sem).start()
        return carry

    lax.fori_loop(0, tq, issue, 0, unroll=8)
    for _ in range(TOP_K):
        pltpu.make_async_copy(x_ref, o_ref.at[pl.ds(0, tq), :], sem).wait()


def _dispatch(dest_flat, pad_lo, pad_n, n_active, xn2, n_rows, tm):
    n, d = xn2.shape
    tq = min(512, n)
    return pl.pallas_call(
        functools.partial(_dispatch_kernel, tm=tm),
        grid_spec=pltpu.PrefetchScalarGridSpec(
            num_scalar_prefetch=4,
            grid=(n // tq,),
            in_specs=[pl.BlockSpec((tq, d), lambda i, *_: (i, 0))],
            out_specs=pl.BlockSpec(memory_space=pl.ANY),
            scratch_shapes=[pltpu.VMEM((tm, d), F32), pltpu.SemaphoreType.DMA(()), pltpu.SemaphoreType.DMA(())],
        ),
        out_shape=jax.ShapeDtypeStruct((n_rows, d), F32),
        compiler_params=_cparams(("arbitrary",)),
        name="dispatch_rows",
    )(dest_flat, pad_lo, pad_n, n_active, xn2)


GU_CHUNK = 512
DN_CHUNK = 256


F8 = jnp.float8_e4m3fn
F8_MAX = 448.0
TINY = 1e-30


def _quantize_columns(w):
    amax = jnp.maximum(jnp.max(jnp.abs(w), axis=0, keepdims=True), TINY)
    return (w * (F8_MAX / amax)).astype(F8), amax * (1.0 / F8_MAX)


def _quantize_rows(x):
    amax = jnp.maximum(jnp.max(jnp.abs(x), axis=1, keepdims=True), TINY)
    return (x * (F8_MAX / amax)).astype(F8), amax * (1.0 / F8_MAX)


IDLE_STEP = -1
FILL_STEP = -2


def _work_schedule(tiles_per_expert, n_chunks, n_steps, n_tiles):
    e_ids = jnp.arange(N_EXPERTS, dtype=I32)
    steps_e = jnp.where(e_ids < N_EXPERTS - 1, jnp.maximum(tiles_per_expert, n_chunks), tiles_per_expert)
    end_e = n_chunks + jnp.cumsum(steps_e)
    start_e = end_e - steps_e
    first_tile_e = jnp.cumsum(tiles_per_expert) - tiles_per_expert
    w = jnp.arange(n_steps, dtype=I32)[:, None]
    prologue = w[:, 0] < n_chunks
    in_e = jnp.logical_and(w >= start_e[None, :], w < end_e[None, :])
    local_e = w - start_e[None, :]
    tile_e = jnp.logical_and(in_e, local_e < tiles_per_expert[None, :])
    conv_e = jnp.logical_and(jnp.logical_and(in_e, local_e < n_chunks), e_ids[None, :] < N_EXPERTS - 1)
    has_tile = jnp.any(tile_e, axis=1)
    has_conv = jnp.logical_or(prologue, jnp.any(conv_e, axis=1))
    tile_id = jnp.sum(jnp.where(tile_e, first_tile_e[None, :] + local_e, 0), axis=1)
    cur_expert = jnp.sum(jnp.where(in_e, e_ids[None, :], 0), axis=1)
    cur_expert = jnp.where(w[:, 0] >= end_e[N_EXPERTS - 1], N_EXPERTS - 1, cur_expert)
    n_used = jnp.sum(tiles_per_expert)
    spare = w[:, 0] - end_e[N_EXPERTS - 1]
    fill = jnp.logical_and(spare >= 0, n_used + spare < n_tiles)
    tiles_done = jnp.sum(jnp.clip(w + 1 - start_e[None, :], 0, tiles_per_expert[None, :]), axis=1)
    fills_done = jnp.clip(spare + 1, 0, n_tiles - n_used)
    convs_done = jnp.clip(w[:, 0] + 1, 0, n_chunks) + jnp.sum(
        jnp.where(e_ids[None, :] < N_EXPERTS - 1, jnp.clip(w + 1 - start_e[None, :], 0, n_chunks), 0), axis=1)
    in_blk = jnp.maximum(tiles_done - 1, 0)
    out_blk = jnp.maximum(tiles_done + fills_done - 1, 0)
    conv_lin = jnp.maximum(convs_done - 1, 0)
    tile = jnp.where(has_tile, tile_id, jnp.where(fill, FILL_STEP, IDLE_STEP))
    return (tile.astype(I32), in_blk.astype(I32), out_blk.astype(I32), has_conv.astype(I32),
            (conv_lin // n_chunks).astype(I32), (conv_lin % n_chunks).astype(I32), (cur_expert % 2).astype(I32),
            cur_expert.astype(I32))


def _gate_up_kernel(tile_ref, iblk_ref, oblk_ref, conv_ref, cexp_ref, cchunk_ref, slot_ref, bexp_ref,
                    x_ref, w32_ref, bg_ref, bl_ref, o_ref, wbuf_ref, wscale_ref):
    del iblk_ref, oblk_ref, bexp_ref
    step = pl.program_id(0)

    @pl.when(tile_ref[step] == FILL_STEP)
    def _():
        o_ref[...] = jnp.zeros_like(o_ref)

    half = GU_CHUNK // 2
    n_slabs = 2 * D_FF // GU_CHUNK

    @pl.when(conv_ref[step] == 1)
    def _():
        src = lax.broadcasted_iota(I32, (half, half), 0)
        dst = lax.broadcasted_iota(I32, (half, half), 1)
        perm = (src == jnp.where(dst < half // 2, 2 * dst, 2 * (dst - half // 2) + 1)).astype(F32)
        w8, scale = _quantize_columns(w32_ref[...])
        rows = jnp.broadcast_to(scale, (8, GU_CHUNK))
        s_hi = rows.astype(BF16)
        rest = rows - s_hi.astype(F32)
        s_mid = rest.astype(BF16)
        s_lo = (rest - s_mid.astype(F32)).astype(BF16)
        w_parts, s_parts = [], []
        for j in range(2):
            cols = slice(j * half, (j + 1) * half)
            w_parts.append(jnp.dot(w8[:, cols], perm.astype(F8), preferred_element_type=F32).astype(F8))
            s_parts.append(sum(jnp.dot(piece[:, cols], perm.astype(BF16), preferred_element_type=F32)
                               for piece in (s_hi, s_mid, s_lo))[0:1])
        cslot = cexp_ref[step] % 2
        c = cchunk_ref[step]
        for slab, lo in ((c, 0), (n_slabs + c, half // 2)):
            wbuf_ref[cslot, slab] = jnp.concatenate([p[:, lo:lo + half // 2] for p in w_parts], axis=1)
            wscale_ref[cslot, slab] = jnp.concatenate([p[:, lo:lo + half // 2] for p in s_parts], axis=1)

    @pl.when(tile_ref[step] >= 0)
    def _():
        slot = slot_ref[step]
        x, x_scale = _quantize_rows(x_ref[...])
        per = 2
        for c in range(0, n_slabs, per):
            sl = slice(c * half, (c + per) * half)
            wg = jnp.concatenate([wbuf_ref[slot, c + j] for j in range(per)], axis=1)
            wl = jnp.concatenate([wbuf_ref[slot, n_slabs + c + j] for j in range(per)], axis=1)
            sg = jnp.concatenate([wscale_ref[slot, c + j] for j in range(per)], axis=1)
            sl_scale = jnp.concatenate([wscale_ref[slot, n_slabs + c + j] for j in range(per)], axis=1)
            gate = jnp.dot(x, wg, preferred_element_type=F32) * (x_scale * sg) + bg_ref[:, sl]
            lin = jnp.dot(x, wl, preferred_element_type=F32) * (x_scale * sl_scale) + bl_ref[:, sl]
            gate = jnp.minimum(gate, SWIGLU_LIMIT)
            lin = jnp.clip(lin, -SWIGLU_LIMIT, SWIGLU_LIMIT)
            o_ref[:, sl] = (gate * _sigmoid(SWIGLU_ALPHA * gate) * (lin + 1.0)).astype(o_ref.dtype)


def _down_kernel(tile_ref, iblk_ref, oblk_ref, conv_ref, cexp_ref, cchunk_ref, slot_ref, bexp_ref,
                 a_ref, w32_ref, bd_ref, o_ref, wbuf_ref, wscale_ref):
    del iblk_ref, oblk_ref, bexp_ref
    step = pl.program_id(0)

    @pl.when(tile_ref[step] == FILL_STEP)
    def _():
        o_ref[...] = jnp.zeros_like(o_ref)


    @pl.when(conv_ref[step] == 1)
    def _():
        w8, scale = _quantize_columns(w32_ref[...])
        cslot = cexp_ref[step] % 2
        wbuf_ref[cslot, cchunk_ref[step]] = w8
        wscale_ref[cslot, cchunk_ref[step]] = scale

    @pl.when(tile_ref[step] >= 0)
    def _():
        slot = slot_ref[step]
        a8, a_scale = _quantize_rows(a_ref[...].astype(F32))
        for c in range(D_MODEL // DN_CHUNK):
            sl = slice(c * DN_CHUNK, (c + 1) * DN_CHUNK)
            y = jnp.dot(a8, wbuf_ref[slot, c], preferred_element_type=F32)
            o_ref[:, sl] = y * (a_scale * wscale_ref[slot, c]) + bd_ref[:, sl]


def _expert_ffn(tiles_per_expert, xs, wgu, bg, bl, wd, bd, tm):
    n_rows, d = xs.shape
    n_tiles = n_rows // tm
    f = wd.shape[1]
    rows_in = lambda w: pl.BlockSpec((tm, w), lambda s, tile, iblk, *_: (iblk[s], 0))
    rows_out = lambda w: pl.BlockSpec((tm, w), lambda s, tile, iblk, oblk, *_: (oblk[s], 0))
    bias = lambda w: pl.BlockSpec((None, 1, w), lambda s, *p: (p[7][s], 0, 0))

    n_chunks = 2 * f // GU_CHUNK
    assert n_chunks == d // DN_CHUNK
    n_steps = n_chunks + n_tiles + n_chunks * N_EXPERTS
    schedule = _work_schedule(tiles_per_expert, n_chunks, n_steps, n_tiles)
    act = pl.pallas_call(
        _gate_up_kernel,
        grid_spec=pltpu.PrefetchScalarGridSpec(
            num_scalar_prefetch=8,
            grid=(n_steps,),
            in_specs=[rows_in(d),
                      pl.BlockSpec((None, d, GU_CHUNK),
                                   lambda s, t, i, o, c, cexp, cchunk, *_: (cexp[s], 0, cchunk[s])),
                      bias(f), bias(f)],
            out_specs=rows_out(f),
            scratch_shapes=[pltpu.VMEM((2, 2 * n_chunks, d, GU_CHUNK // 2), F8),
                            pltpu.VMEM((2, 2 * n_chunks, 1, GU_CHUNK // 2), F32)],
        ),
        out_shape=jax.ShapeDtypeStruct((n_rows, f), BF16),
        compiler_params=_cparams(("arbitrary",)),
        name="expert_gate_up",
    )(*schedule, xs, wgu, bg, bl)

    return pl.pallas_call(
        _down_kernel,
        grid_spec=pltpu.PrefetchScalarGridSpec(
            num_scalar_prefetch=8,
            grid=(n_steps,),
            in_specs=[rows_in(f),
                      pl.BlockSpec((None, f, DN_CHUNK),
                                   lambda s, t, i, o, c, cexp, cchunk, *_: (cexp[s], 0, cchunk[s])),
                      bias(d)],
            out_specs=rows_out(d),
            scratch_shapes=[pltpu.VMEM((2, n_chunks, f, DN_CHUNK), F8),
                            pltpu.VMEM((2, n_chunks, 1, DN_CHUNK), F32)],
        ),
        out_shape=jax.ShapeDtypeStruct((n_rows, d), F32),
        compiler_params=_cparams(("arbitrary",)),
        name="expert_down",
    )(*schedule, act, wd, bd)


def _combine_kernel(dest_ref, y_ref, tw_ref, h1_ref, fw_ref, o_ref, buf_ref, sem, *, final_norm):
    i = pl.program_id(0)
    steps = pl.num_programs(0)
    tq = h1_ref.shape[0]
    n = steps * tq

    def issue(step, slot):
        def body(r, carry):
            for k in range(TOP_K):
                src = dest_ref[k * n + step * tq + r]
                pltpu.make_async_copy(y_ref.at[pl.ds(src, 1), :], buf_ref.at[slot, k, pl.ds(r, 1), :],
                                      sem.at[slot]).start()
            return carry
        lax.fori_loop(0, tq, body, 0, unroll=8)

    @pl.when(i == 0)
    def _():
        issue(0, 0)

    @pl.when(i + 1 < steps)
    def _():
        issue(i + 1, (i + 1) % 2)

    slot = i % 2
    for k in range(TOP_K):
        pltpu.make_async_copy(y_ref.at[pl.ds(0, tq), :], buf_ref.at[slot, k], sem.at[slot]).wait()

    acc = h1_ref[...]
    moe = tw_ref[:, 0:1] * buf_ref[slot, 0]
    for k in range(1, TOP_K):
        moe = moe + tw_ref[:, k:k + 1] * buf_ref[slot, k]
    acc = acc + moe
    if final_norm:
        acc = acc * lax.rsqrt(jnp.mean(acc * acc, axis=-1, keepdims=True) + RMS_EPS) * fw_ref[...]
    o_ref[...] = acc


def _combine(dest_flat, y, top_w_t, h1, final_w, final_norm):
    n, d = h1.shape
    tq = min(256, n)
    return pl.pallas_call(
        functools.partial(_combine_kernel, final_norm=final_norm),
        grid_spec=pltpu.PrefetchScalarGridSpec(
            num_scalar_prefetch=1,
            grid=(n // tq,),
            in_specs=[pl.BlockSpec(memory_space=pl.ANY),
                      pl.BlockSpec((tq, TOP_K), lambda i, dest: (i, 0)),
                      pl.BlockSpec((tq, d), lambda i, dest: (i, 0)),
                      pl.BlockSpec((1, d), lambda i, dest: (0, 0))],
            out_specs=pl.BlockSpec((tq, d), lambda i, dest: (i, 0)),
            scratch_shapes=[pltpu.VMEM((2, TOP_K, tq, d), F32), pltpu.SemaphoreType.DMA((2,))],
        ),
        out_shape=jax.ShapeDtypeStruct((n, d), F32),
        compiler_params=_cparams(("arbitrary",)),
        name="combine_rows",
    )(dest_flat, y, top_w_t, h1, final_w.reshape(1, d))


def _cast_kernel(w_ref, o_ref):
    o_ref[...] = w_ref[...].astype(o_ref.dtype)


def _cast_in_weights(w):
    d, width = w.shape
    tn = 2 * SWA_KV_W
    kv_blk = (2 * RET_QK_W + 2 * RET_V_W + SWA_Q_W) // tn
    n_blk = width // tn

    def source_block(j):
        return jnp.where(j < kv_blk, j, jnp.where(j < n_blk - 1, j + 1, kv_blk))

    return pl.pallas_call(
        _cast_kernel,
        grid=(n_blk,),
        in_specs=[pl.BlockSpec((d, tn), lambda j: (0, source_block(j)))],
        out_specs=pl.BlockSpec((d, tn), lambda j: (0, j)),
        out_shape=jax.ShapeDtypeStruct((d, width), BF16),
        compiler_params=_cparams(("parallel",)),
        name="cast_in_weights",
    )(w)


def _layer(h, tables, p, final_w, final_norm, expert_tile):
    b, s, d = h.shape
    n = b * s
    cr, sr, cs, sa, sb = tables
    x2 = h.reshape(n, d)
    proj = _in_projection(x2, p["attn_norm_w"], _cast_in_weights(p["w_in"]))
    proj3 = proj.reshape(b, s, IN_WIDTH)
    t3 = lambda t: t.reshape(b, s, LANES)
    ret = _retention(proj3, t3(cr), t3(sr))
    swa = _sliding_window(proj3, p["sinks"], t3(cs), t3(sa), t3(sb))
    h1, xn2, top_idx, top_w, rank, counts = _post_attention(
        ret.reshape(n, RET_V_W), swa.reshape(n, SWA_Q_W), proj, x2, p["gate_bias"],
        p["w_ret_out"].astype(BF16), p["w_swa_out"].astype(BF16), p["w_o"].astype(BF16),
        p["ffn_norm_w"], p["router_w"], p["router_b"])

    tm = expert_tile
    n_tiles = (n * TOP_K) // tm + N_EXPERTS
    cnt = counts[:, 0].astype(I32)
    padded = ((cnt + tm - 1) // tm) * tm
    ends = jnp.cumsum(padded)
    group_start = ends - padded
    n_active = (ends[-1] // tm).reshape(1)

    dest_flat = _dest_rows(top_idx, rank, group_start).reshape(TOP_K * n)
    xs = _dispatch(dest_flat, group_start + cnt, padded - cnt, n_active, xn2, n_tiles * tm, tm)

    bgu = p["b_gate_up"]
    y = _expert_ffn(padded // tm, xs, p["w_gate_up"], bgu[:, 0::2].reshape(N_EXPERTS, 1, D_FF),
                    bgu[:, 1::2].reshape(N_EXPERTS, 1, D_FF), p["w_down"], p["b_down"].reshape(N_EXPERTS, 1, d), tm)
    out = _combine(dest_flat, y, top_w.T, h1, final_w, final_norm)
    return out.reshape(b, s, d)


def kernel(x, positions, attn_norm_w, w_in, gate_bias, w_ret_out, w_swa_out, w_o, sinks, ffn_norm_w, router_w,
           router_b, w_gate_up, b_gate_up, w_down, b_down, final_norm_w):
    depth = w_in.shape[0]
    stacked = dict(attn_norm_w=attn_norm_w, w_in=w_in, gate_bias=gate_bias, w_ret_out=w_ret_out,
                   w_swa_out=w_swa_out, w_o=w_o, sinks=sinks, ffn_norm_w=ffn_norm_w, router_w=router_w,
                   router_b=router_b, w_gate_up=w_gate_up, b_gate_up=b_gate_up, w_down=w_down, b_down=b_down)
    tables = _rope_tables(positions)
    h = x
    for layer in range(depth):
        p = {name: w[layer] for name, w in stacked.items()}
        h = _layer(h, tables, p, final_norm_w, layer == depth - 1, expert_tile=256)
    return h
```

```python
import functools
import math

import jax
import jax.numpy as jnp
from jax import lax
from jax.experimental import pallas as pl
from jax.experimental.pallas import tpu as pltpu

F32 = jnp.float32
BF16 = jnp.bfloat16
I32 = jnp.int32

D_MODEL = 2048
RMS_EPS = 1e-5
ROPE_THETA = 10000.0
RET_HEADS = 8
RET_DK = 128
RET_DV = 256
RET_CHUNK = 128
SWA_Q_HEADS = 32
SWA_KV_HEADS = 4
SWA_HEAD_DIM = 64
SWA_WINDOW = 128
SWA_BLOCK = 128
N_EXPERTS = 32
TOP_K = 4
D_FF = 2048
SWIGLU_LIMIT = 7.0
SWIGLU_ALPHA = 1.702

RET_QK_W = RET_HEADS * RET_DK
RET_V_W = RET_HEADS * RET_DV
SWA_Q_W = SWA_Q_HEADS * SWA_HEAD_DIM
SWA_KV_W = SWA_KV_HEADS * SWA_HEAD_DIM
GATE_W = 2 * D_MODEL
IN_WIDTH = 2 * RET_QK_W + 2 * RET_V_W + SWA_Q_W + 2 * SWA_KV_W + GATE_W

LANES = 128
VMEM_LIMIT = 56 * 1024 * 1024
NEG_BIG = -1e30

_NT = (((1,), (1,)), ((), ()))
_TN = (((0,), (0,)), ((), ()))


def _cparams(sem):
    return pltpu.CompilerParams(dimension_semantics=sem, vmem_limit_bytes=VMEM_LIMIT)


def _sigmoid(x):
    return 1.0 / (1.0 + jnp.exp(-x))


def _rope_table_kernel(pos_ref, invr_ref, invs_ref, cr_ref, sr_ref, cs_ref, sa_ref, sb_ref):
    pos = pos_ref[...]
    lane = lax.broadcasted_iota(I32, (pos.shape[0], LANES), 1)
    ang_r = pos * invr_ref[...]
    cr_ref[...] = jnp.cos(ang_r)
    sr_ref[...] = jnp.where(lane < RET_DK // 2, -1.0, 1.0) * jnp.sin(ang_r)
    ang_s = pos * invs_ref[...]
    sin_s = jnp.sin(ang_s)
    first_half = (lane % SWA_HEAD_DIM) < SWA_HEAD_DIM // 2
    cs_ref[...] = jnp.cos(ang_s)
    sa_ref[...] = jnp.where(first_half, -sin_s, 0.0)
    sb_ref[...] = jnp.where(first_half, 0.0, sin_s)


def _rope_tables(positions):
    n = positions.size
    pos = positions.reshape(n, 1).astype(F32)
    inv_r = 1.0 / (ROPE_THETA ** (jnp.arange(0, RET_DK, 2, dtype=F32) / RET_DK))
    inv_s = 1.0 / (ROPE_THETA ** (jnp.arange(0, SWA_HEAD_DIM, 2, dtype=F32) / SWA_HEAD_DIM))
    inv_r = jnp.tile(inv_r, LANES // inv_r.size).reshape(1, LANES)
    inv_s = jnp.tile(inv_s, LANES // inv_s.size).reshape(1, LANES)
    t = min(2048, n)
    tab = pl.BlockSpec((t, LANES), lambda i: (i, 0))
    row = pl.BlockSpec((1, LANES), lambda i: (0, 0))
    return pl.pallas_call(
        _rope_table_kernel,
        grid=(n // t,),
        in_specs=[pl.BlockSpec((t, 1), lambda i: (i, 0)), row, row],
        out_specs=[tab] * 5,
        out_shape=[jax.ShapeDtypeStruct((n, LANES), F32)] * 5,
        compiler_params=_cparams(("parallel",)),
        name="rope_tables",
    )(pos, inv_r, inv_s)


def _inproj_kernel(x_ref, nw_ref, w_ref, o_ref, xn_ref):
    @pl.when(pl.program_id(1) == 0)
    def _():
        x = x_ref[...]
        ms = jnp.mean(x * x, axis=-1, keepdims=True)
        xn_ref[...] = (x * lax.rsqrt(ms + RMS_EPS) * nw_ref[...]).astype(BF16)

    o_ref[...] = jnp.dot(xn_ref[...], w_ref[...], preferred_element_type=F32).astype(o_ref.dtype)


def _in_projection(x2, norm_w, w_in_bf16):
    n, d = x2.shape
    width = w_in_bf16.shape[1]
    tm = min(1024, n)
    tn = 1280
    return pl.pallas_call(
        _inproj_kernel,
        grid=(n // tm, width // tn),
        in_specs=[pl.BlockSpec((tm, d), lambda i, j: (i, 0)),
                  pl.BlockSpec((1, d), lambda i, j: (0, 0)),
                  pl.BlockSpec((d, tn), lambda i, j: (0, j))],
        out_specs=pl.BlockSpec((tm, tn), lambda i, j: (i, j)),
        out_shape=jax.ShapeDtypeStruct((n, width), BF16),
        scratch_shapes=[pltpu.VMEM((tm, d), BF16)],
        compiler_params=_cparams(("parallel", "arbitrary")),
        name="in_projection",
    )(x2, norm_w.reshape(1, d), w_in_bf16)


def _retention_kernel(q_ref, k_ref, v_ref, g_ref, cr_ref, sr_ref, o_ref, st_ref, decay_ref, xi_ref, zeta_ref):
    c = RET_CHUNK

    @pl.when(jnp.logical_and(pl.program_id(0) == 0, pl.program_id(1) == 0))
    def _():
        row = lax.broadcasted_iota(I32, (c, c), 0)
        col = lax.broadcasted_iota(I32, (c, c), 1)
        rel = (row - col).astype(F32)
        n_row = row.astype(F32)
        scale = RET_DK ** -0.5
        for h in range(RET_HEADS):
            log_gamma = math.log1p(-(2.0 ** (-5.0 - h)))
            decay_ref[h] = jnp.where(rel >= 0, jnp.exp(log_gamma * jnp.maximum(rel, 0.0)), 0.0) * scale
            xi_ref[h] = jnp.exp(log_gamma * (n_row + 1.0))
            zeta_ref[h] = jnp.exp(log_gamma * (c - 1.0 - n_row)) * scale

    @pl.when(pl.program_id(1) == 0)
    def _():
        st_ref[...] = jnp.zeros_like(st_ref)

    cr = cr_ref[...]
    sr = sr_ref[...]
    for h in range(RET_HEADS):
        log_gamma = math.log1p(-(2.0 ** (-5.0 - h)))
        q = q_ref[:, h * RET_DK:(h + 1) * RET_DK].astype(F32)
        k = k_ref[:, h * RET_DK:(h + 1) * RET_DK].astype(F32)
        qr = q * cr + pltpu.roll(q, RET_DK // 2, 1) * sr
        kr = k * cr + pltpu.roll(k, RET_DK // 2, 1) * sr
        s = lax.dot_general(qr.astype(BF16), kr.astype(BF16), _NT, preferred_element_type=F32) * decay_ref[h]
        v = v_ref[:, h * RET_DV:(h + 1) * RET_DV]
        inner = jnp.dot(s.astype(BF16), v, preferred_element_type=F32)
        st = st_ref[h]
        cross = jnp.dot((qr * xi_ref[h]).astype(BF16), st.astype(BF16), preferred_element_type=F32)
        o = inner + cross
        u = lax.dot_general((kr * zeta_ref[h]).astype(BF16), v, _TN, preferred_element_type=F32)
        st_ref[h] = math.exp(log_gamma * c) * st + u
        o = o * lax.rsqrt(jnp.mean(o * o, axis=-1, keepdims=True) + RMS_EPS)
        g = g_ref[:, h * RET_DV:(h + 1) * RET_DV].astype(F32)
        o_ref[:, h * RET_DV:(h + 1) * RET_DV] = (g * _sigmoid(g) * o).astype(o_ref.dtype)


def _retention(proj3, cr3, sr3):
    b, s, _ = proj3.shape
    c = RET_CHUNK
    return pl.pallas_call(
        _retention_kernel,
        grid=(b, s // c),
        in_specs=[pl.BlockSpec((None, c, RET_QK_W), lambda i, j: (i, j, 0)),
                  pl.BlockSpec((None, c, RET_QK_W), lambda i, j: (i, j, 1)),
                  pl.BlockSpec((None, c, RET_V_W), lambda i, j: (i, j, 1)),
                  pl.BlockSpec((None, c, RET_V_W), lambda i, j: (i, j, 2)),
                  pl.BlockSpec((None, c, LANES), lambda i, j: (i, j, 0)),
                  pl.BlockSpec((None, c, LANES), lambda i, j: (i, j, 0))],
        out_specs=pl.BlockSpec((None, c, RET_V_W), lambda i, j: (i, j, 0)),
        out_shape=jax.ShapeDtypeStruct((b, s, RET_V_W), BF16),
        scratch_shapes=[pltpu.VMEM((RET_HEADS, RET_DK, RET_DV), F32)]
        + [pltpu.VMEM((RET_HEADS, c, c), F32)] * 3,
        compiler_params=_cparams(("arbitrary", "arbitrary")),
        name="retention",
    )(proj3, proj3, proj3, proj3, cr3, sr3)


def _swa_kernel(sinks_ref, q_ref, kv_ref, kvp_ref, cs_ref, sa_ref, sb_ref, csp_ref, sap_ref, sbp_ref, o_ref):
    blk = pl.program_id(1)
    wb = SWA_BLOCK
    hd = SWA_HEAD_DIM
    group = SWA_Q_HEADS // SWA_KV_HEADS

    def rope(x, cs, sa, sb):
        return x * cs + pltpu.roll(x, LANES - hd // 2, 1) * sa + pltpu.roll(x, hd // 2, 1) * sb

    def rope_keys(ref, tabs):
        return [rope(ref[:, j * LANES:(j + 1) * LANES].astype(F32), *tabs) for j in range(SWA_KV_W // LANES)]

    cur_tabs = (cs_ref[...], sa_ref[...], sb_ref[...])
    kc = rope_keys(kv_ref, cur_tabs)
    kp = rope_keys(kvp_ref, (csp_ref[...], sap_ref[...], sbp_ref[...]))
    vc = kv_ref[:, SWA_KV_W:2 * SWA_KV_W].astype(F32)
    vp = kvp_ref[:, SWA_KV_W:2 * SWA_KV_W].astype(F32)

    qr = [rope(q_ref[:, j * LANES:(j + 1) * LANES].astype(F32), *cur_tabs) * (hd ** -0.5)
          for j in range(SWA_Q_W // LANES)]

    def head_cols(slabs, h):
        return slabs[h // 2][:, (h % 2) * hd:(h % 2 + 1) * hd]

    cols = group * wb
    kj = lax.broadcasted_iota(I32, (2 * wb, cols), 0)
    qi = lax.broadcasted_iota(I32, (2 * wb, cols), 1) & (wb - 1)
    valid = (kj > qi) & (kj <= qi + SWA_WINDOW) & ((kj >= wb) | (blk > 0))

    for kh in range(SWA_KV_HEADS):
        kband = jnp.concatenate([head_cols(kp, kh), head_cols(kc, kh)], axis=0).astype(BF16)
        vband = jnp.concatenate([vp[:, kh * hd:(kh + 1) * hd], vc[:, kh * hd:(kh + 1) * hd]], axis=0).astype(BF16)
        heads = [kh * group + g for g in range(group)]
        qg = jnp.concatenate([head_cols(qr, h) for h in heads], axis=0).astype(BF16)
        sink = jnp.concatenate([jnp.full((1, wb), sinks_ref[h], F32) for h in heads], axis=1)
        s = lax.dot_general(kband, qg, _NT, preferred_element_type=F32)
        s = jnp.where(valid, s, NEG_BIG)
        m = jnp.maximum(jnp.max(s, axis=0, keepdims=True), sink)
        p = jnp.exp(s - m)
        denom = jnp.sum(p, axis=0, keepdims=True) + jnp.exp(sink - m)
        o = lax.dot_general((p * (1.0 / denom)).astype(BF16), vband, _TN, preferred_element_type=F32)
        for g in range(0, group, 2):
            pair = jnp.concatenate([o[g * wb:(g + 1) * wb], o[(g + 1) * wb:(g + 2) * wb]], axis=1)
            h = heads[g]
            o_ref[:, h * hd:(h + 2) * hd] = pair.astype(o_ref.dtype)


def _sliding_window(proj3, sinks, cs3, sa3, sb3):
    b, s, _ = proj3.shape
    wb = SWA_BLOCK
    q_blk = (2 * RET_QK_W + 2 * RET_V_W) // SWA_Q_W
    kv_blk = (2 * RET_QK_W + 2 * RET_V_W + SWA_Q_W + GATE_W) // (2 * SWA_KV_W)
    prev = lambda j: jnp.maximum(j - 1, 0)
    tab = pl.BlockSpec((None, wb, LANES), lambda i, j: (i, j, 0))
    tab_prev = pl.BlockSpec((None, wb, LANES), lambda i, j: (i, prev(j), 0))
    return pl.pallas_call(
        _swa_kernel,
        grid=(b, s // wb),
        in_specs=[pl.BlockSpec(memory_space=pltpu.SMEM),
                  pl.BlockSpec((None, wb, SWA_Q_W), lambda i, j: (i, j, q_blk)),
                  pl.BlockSpec((None, wb, 2 * SWA_KV_W), lambda i, j: (i, j, kv_blk)),
                  pl.BlockSpec((None, wb, 2 * SWA_KV_W), lambda i, j: (i, prev(j), kv_blk)),
                  tab, tab, tab, tab_prev, tab_prev, tab_prev],
        out_specs=pl.BlockSpec((None, wb, SWA_Q_W), lambda i, j: (i, j, 0)),
        out_shape=jax.ShapeDtypeStruct((b, s, SWA_Q_W), BF16),
        compiler_params=_cparams(("parallel", "parallel")),
        name="sliding_window",
    )(sinks, proj3, proj3, proj3, cs3, sa3, sb3, cs3, sa3, sb3)


def _post_kernel(ret_ref, swa_ref, gl_ref, x_ref, gb_ref, wro_ref, wso_ref, wo_ref, fw_ref, rwt_ref, rb_ref,
                 h1_ref, xn2_ref, idx_ref, tw_ref, rank_ref, cnt_ref, carry_ref):
    @pl.when(pl.program_id(0) == 0)
    def _():
        carry_ref[...] = jnp.zeros_like(carry_ref)

    d = D_MODEL
    rows = x_ref.shape[0]
    n_sub = 2
    for r in range(n_sub):
        rs = slice(r * rows // n_sub, (r + 1) * rows // n_sub)
        a = jnp.dot(ret_ref[rs, :], wro_ref[...], preferred_element_type=F32)
        b = jnp.dot(swa_ref[rs, :], wso_ref[...], preferred_element_type=F32)
        gl = gl_ref[rs, :].astype(F32) + gb_ref[...]
        mix = _sigmoid(gl[:, :d]) * a + _sigmoid(gl[:, d:]) * b
        h1 = x_ref[rs, :] + jnp.dot(mix.astype(BF16), wo_ref[...], preferred_element_type=F32)
        h1_ref[rs, :] = h1
        xn2_ref[rs, :] = h1 * lax.rsqrt(jnp.mean(h1 * h1, axis=-1, keepdims=True) + RMS_EPS) * fw_ref[...]
    xn2 = xn2_ref[...]

    xh = xn2.astype(BF16)
    xl = (xn2 - xh.astype(F32)).astype(BF16)
    rw = rwt_ref[...]
    rh = rw.astype(BF16)
    rl = (rw - rh.astype(F32)).astype(BF16)
    logits = (lax.dot_general(rh, xh, _NT, preferred_element_type=F32)
              + lax.dot_general(rh, xl, _NT, preferred_element_type=F32)
              + lax.dot_general(rl, xh, _NT, preferred_element_type=F32)) + rb_ref[...]
    tm = logits.shape[1]
    e_iota = lax.broadcasted_iota(I32, (N_EXPERTS, tm), 0)
    vals, idxs = [], []
    for _ in range(TOP_K):
        m = jnp.max(logits, axis=0, keepdims=True)
        ix = jnp.min(jnp.where(logits == m, e_iota, N_EXPERTS), axis=0, keepdims=True)
        vals.append(m)
        idxs.append(ix)
        logits = jnp.where(e_iota == ix, -jnp.inf, logits)
    ex = [jnp.exp(v - vals[0]) for v in vals]
    den = ex[0] + ex[1] + ex[2] + ex[3]
    tw_ref[...] = jnp.concatenate([e / den for e in ex], axis=0)
    idx_ref[...] = jnp.concatenate(idxs, axis=0)

    onehot = jnp.zeros((N_EXPERTS, tm), F32)
    for ix in idxs:
        onehot = onehot + (e_iota == ix).astype(F32)
    earlier = (lax.broadcasted_iota(I32, (tm, tm), 0) < lax.broadcasted_iota(I32, (tm, tm), 1)).astype(BF16)
    prefix = jnp.dot(onehot.astype(BF16), earlier, preferred_element_type=F32) + carry_ref[:, 0:1]
    ranks = [jnp.sum(jnp.where(e_iota == ix, prefix, 0.0), axis=0, keepdims=True) for ix in idxs]
    rank_ref[...] = jnp.concatenate(ranks, axis=0).astype(I32)
    carry = carry_ref[...] + jnp.sum(onehot, axis=1, keepdims=True)
    carry_ref[...] = carry
    cnt_ref[...] = carry


def _post_attention(ret2, swa2, proj2, x2, gate_bias, wro, wso, wo, ffn_w, router_w, router_b):
    n, d = x2.shape
    tm = min(256, n)
    gl_blk = (2 * RET_QK_W + 2 * RET_V_W + SWA_Q_W) // GATE_W
    const = lambda shape: pl.BlockSpec(shape, lambda i: (0,) * len(shape), pipeline_mode=pl.Buffered(1))
    rowblk = lambda w: pl.BlockSpec((tm, w), lambda i: (i, 0))
    tokrow = pl.BlockSpec((TOP_K, tm), lambda i: (0, i))
    return pl.pallas_call(
        _post_kernel,
        grid=(n // tm,),
        in_specs=[rowblk(RET_V_W), rowblk(SWA_Q_W),
                  pl.BlockSpec((tm, GATE_W), lambda i: (i, gl_blk)),
                  rowblk(d), const((1, GATE_W)),
                  const((RET_V_W, d)), const((SWA_Q_W, d)), const((d, d)),
                  const((1, d)), const((N_EXPERTS, d)), const((N_EXPERTS, 1))],
        out_specs=[rowblk(d), rowblk(d), tokrow, tokrow, tokrow,
                   pl.BlockSpec((N_EXPERTS, LANES), lambda i: (0, 0))],
        out_shape=[jax.ShapeDtypeStruct((n, d), F32), jax.ShapeDtypeStruct((n, d), F32),
                   jax.ShapeDtypeStruct((TOP_K, n), I32), jax.ShapeDtypeStruct((TOP_K, n), F32),
                   jax.ShapeDtypeStruct((TOP_K, n), I32), jax.ShapeDtypeStruct((N_EXPERTS, LANES), F32)],
        scratch_shapes=[pltpu.VMEM((N_EXPERTS, LANES), F32)],
        compiler_params=_cparams(("arbitrary",)),
        name="post_attention",
    )(ret2, swa2, proj2, x2, gate_bias.reshape(1, GATE_W), wro, wso, wo, ffn_w.reshape(1, d),
      router_w.T, router_b.reshape(N_EXPERTS, 1))


def _dest_kernel(idx_ref, rank_ref, gs_ref, o_ref):
    t = idx_ref.shape[1]
    e_iota = lax.broadcasted_iota(I32, (N_EXPERTS, t), 0)
    gs = gs_ref[...]
    rows = []
    for k in range(TOP_K):
        start = jnp.sum(jnp.where(e_iota == idx_ref[k:k + 1, :], gs, 0), axis=0, keepdims=True)
        rows.append(start + rank_ref[k:k + 1, :])
    o_ref[...] = jnp.concatenate(rows, axis=0)


def _dest_rows(top_idx, rank, group_start):
    n = top_idx.shape[1]
    t = min(2048, n)
    blk = pl.BlockSpec((TOP_K, t), lambda i: (0, i))
    return pl.pallas_call(
        _dest_kernel,
        grid=(n // t,),
        in_specs=[blk, blk, pl.BlockSpec((N_EXPERTS, 1), lambda i: (0, 0))],
        out_specs=blk,
        out_shape=jax.ShapeDtypeStruct((TOP_K, n), I32),
        compiler_params=_cparams(("parallel",)),
        name="dest_rows",
    )(top_idx, rank, group_start.reshape(N_EXPERTS, 1))


def _row_copy(src, s, dst, d, sem):
    return pltpu.make_async_copy(src.at[pl.ds(s, 1), :], dst.at[pl.ds(d, 1), :], sem)


def _dispatch_kernel(dest_ref, pad_lo_ref, pad_n_ref, na_ref, x_ref, o_ref, zero_ref, sem, zsem, *, tm):
    i = pl.program_id(0)
    tq = x_ref.shape[0]
    n = pl.num_programs(0) * tq
    n_tiles = o_ref.shape[0] // tm

    @pl.when(i == 0)
    def _():
        zero_ref[...] = jnp.zeros_like(zero_ref)

        def pad_rows(e, total):
            def one(r, carry):
                _row_copy(zero_ref, 0, o_ref, pad_lo_ref[e] + r, zsem).start()
                return carry
            lax.fori_loop(0, pad_n_ref[e], one, 0)
            return total + pad_n_ref[e]

        n_pad = lax.fori_loop(0, N_EXPERTS, pad_rows, 0)

        def tile_copy(t):
            return pltpu.make_async_copy(zero_ref, o_ref.at[pl.ds(t * tm, tm), :], zsem)

        def start_tile(t, carry):
            tile_copy(t).start()
            return carry

        def drain_tile(t, carry):
            tile_copy(0).wait()
            return carry

        def drain_row(r, carry):
            _row_copy(zero_ref, 0, o_ref, 0, zsem).wait()
            return carry

        lax.fori_loop(na_ref[0], n_tiles, start_tile, 0)
        lax.fori_loop(0, n_pad, drain_row, 0)
        lax.fori_loop(na_ref[0], n_tiles, drain_tile, 0)

    def issue(r, carry):
        for k in range(TOP_K):
            _row_copy(x_ref, r, o_ref, dest_ref[k * n + i * tq + r], sem).start()
        return carry

    lax.fori_loop(0, tq, issue, 0, unroll=8)
    for _ in range(TOP_K):
        pltpu.make_async_copy(x_ref, o_ref.at[pl.ds(0, tq), :], sem).wait()


def _dispatch(dest_flat, pad_lo, pad_n, n_active, xn2, n_rows, tm):
    n, d = xn2.shape
    tq = min(512, n)
    return pl.pallas_call(
        functools.partial(_dispatch_kernel, tm=tm),
        grid_spec=pltpu.PrefetchScalarGridSpec(
            num_scalar_prefetch=4,
            grid=(n // tq,),
            in_specs=[pl.BlockSpec((tq, d), lambda i, *_: (i, 0))],
            out_specs=pl.BlockSpec(memory_space=pl.ANY),
            scratch_shapes=[pltpu.VMEM((tm, d), F32), pltpu.SemaphoreType.DMA(()), pltpu.SemaphoreType.DMA(())],
        ),
        out_shape=jax.ShapeDtypeStruct((n_rows, d), F32),
        compiler_params=_cparams(("arbitrary",)),
        name="dispatch_rows",
    )(dest_flat, pad_lo, pad_n, n_active, xn2)


GU_CHUNK = 512
DN_CHUNK = 256


F8 = jnp.float8_e4m3fn
F8_MAX = 448.0
TINY = 1e-30


def _quantize_columns(w):
    amax = jnp.maximum(jnp.max(jnp.abs(w), axis=0, keepdims=True), TINY)
    return (w * (F8_MAX / amax)).astype(F8), amax * (1.0 / F8_MAX)


def _quantize_rows(x):
    amax = jnp.maximum(jnp.max(jnp.abs(x), axis=1, keepdims=True), TINY)
    return (x * (F8_MAX / amax)).astype(F8), amax * (1.0 / F8_MAX)


IDLE_STEP = -1
FILL_STEP = -2


def _work_schedule(tiles_per_expert, n_chunks, n_steps, n_tiles):
    e_ids = jnp.arange(N_EXPERTS, dtype=I32)
    steps_e = jnp.where(e_ids < N_EXPERTS - 1, jnp.maximum(tiles_per_expert, n_chunks), tiles_per_expert)
    end_e = n_chunks + jnp.cumsum(steps_e)
    start_e = end_e - steps_e
    first_tile_e = jnp.cumsum(tiles_per_expert) - tiles_per_expert
    w = jnp.arange(n_steps, dtype=I32)[:, None]
    prologue = w[:, 0] < n_chunks
    in_e = jnp.logical_and(w >= start_e[None, :], w < end_e[None, :])
    local_e = w - start_e[None, :]
    tile_e = jnp.logical_and(in_e, local_e < tiles_per_expert[None, :])
    conv_e = jnp.logical_and(jnp.logical_and(in_e, local_e < n_chunks), e_ids[None, :] < N_EXPERTS - 1)
    has_tile = jnp.any(tile_e, axis=1)
    has_conv = jnp.logical_or(prologue, jnp.any(conv_e, axis=1))
    tile_id = jnp.sum(jnp.where(tile_e, first_tile_e[None, :] + local_e, 0), axis=1)
    cur_expert = jnp.sum(jnp.where(in_e, e_ids[None, :], 0), axis=1)
    cur_expert = jnp.where(w[:, 0] >= end_e[N_EXPERTS - 1], N_EXPERTS - 1, cur_expert)
    n_used = jnp.sum(tiles_per_expert)
    spare = w[:, 0] - end_e[N_EXPERTS - 1]
    fill = jnp.logical_and(spare >= 0, n_used + spare < n_tiles)
    tiles_done = jnp.sum(jnp.clip(w + 1 - start_e[None, :], 0, tiles_per_expert[None, :]), axis=1)
    fills_done = jnp.clip(spare + 1, 0, n_tiles - n_used)
    convs_done = jnp.clip(w[:, 0] + 1, 0, n_chunks) + jnp.sum(
        jnp.where(e_ids[None, :] < N_EXPERTS - 1, jnp.clip(w + 1 - start_e[None, :], 0, n_chunks), 0), axis=1)
    in_blk = jnp.maximum(tiles_done - 1, 0)
    out_blk = jnp.maximum(tiles_done + fills_done - 1, 0)
    conv_lin = jnp.maximum(convs_done - 1, 0)
    tile = jnp.where(has_tile, tile_id, jnp.where(fill, FILL_STEP, IDLE_STEP))
    return (tile.astype(I32), in_blk.astype(I32), out_blk.astype(I32), has_conv.astype(I32),
            (conv_lin // n_chunks).astype(I32), (conv_lin % n_chunks).astype(I32), (cur_expert % 2).astype(I32),
            cur_expert.astype(I32))


def _gate_up_kernel(tile_ref, iblk_ref, oblk_ref, conv_ref, cexp_ref, cchunk_ref, slot_ref, bexp_ref,
                    x_ref, w32_ref, bg_ref, bl_ref, o_ref, wbuf_ref, wscale_ref):
    del iblk_ref, oblk_ref, bexp_ref
    step = pl.program_id(0)

    @pl.when(tile_ref[step] == FILL_STEP)
    def _():
        o_ref[...] = jnp.zeros_like(o_ref)

    half = GU_CHUNK // 2
    n_slabs = 2 * D_FF // GU_CHUNK

    @pl.when(conv_ref[step] == 1)
    def _():
        src = lax.broadcasted_iota(I32, (half, half), 0)
        dst = lax.broadcasted_iota(I32, (half, half), 1)
        perm = (src == jnp.where(dst < half // 2, 2 * dst, 2 * (dst - half // 2) + 1)).astype(F32)
        w8, scale = _quantize_columns(w32_ref[...])
        rows = jnp.broadcast_to(scale, (8, GU_CHUNK))
        s_hi = rows.astype(BF16)
        rest = rows - s_hi.astype(F32)
        s_mid = rest.astype(BF16)
        s_lo = (rest - s_mid.astype(F32)).astype(BF16)
        w_parts, s_parts = [], []
        for j in range(2):
            cols = slice(j * half, (j + 1) * half)
            w_parts.append(jnp.dot(w8[:, cols], perm.astype(F8), preferred_element_type=F32).astype(F8))
            s_parts.append(sum(jnp.dot(piece[:, cols], perm.astype(BF16), preferred_element_type=F32)
                               for piece in (s_hi, s_mid, s_lo))[0:1])
        cslot = cexp_ref[step] % 2
        c = cchunk_ref[step]
        for slab, lo in ((c, 0), (n_slabs + c, half // 2)):
            wbuf_ref[cslot, slab] = jnp.concatenate([p[:, lo:lo + half // 2] for p in w_parts], axis=1)
            wscale_ref[cslot, slab] = jnp.concatenate([p[:, lo:lo + half // 2] for p in s_parts], axis=1)

    @pl.when(tile_ref[step] >= 0)
    def _():
        slot = slot_ref[step]
        x, x_scale = _quantize_rows(x_ref[...])
        per = 2
        for c in range(0, n_slabs, per):
            sl = slice(c * half, (c + per) * half)
            wg = jnp.concatenate([wbuf_ref[slot, c + j] for j in range(per)], axis=1)
            wl = jnp.concatenate([wbuf_ref[slot, n_slabs + c + j] for j in range(per)], axis=1)
            sg = jnp.concatenate([wscale_ref[slot, c + j] for j in range(per)], axis=1)
            sl_scale = jnp.concatenate([wscale_ref[slot, n_slabs + c + j] for j in range(per)], axis=1)
            gate = jnp.dot(x, wg, preferred_element_type=F32) * (x_scale * sg) + bg_ref[:, sl]
            lin = jnp.dot(x, wl, preferred_element_type=F32) * (x_scale * sl_scale) + bl_ref[:, sl]
            gate = jnp.minimum(gate, SWIGLU_LIMIT)
            lin = jnp.clip(lin, -SWIGLU_LIMIT, SWIGLU_LIMIT)
            o_ref[:, sl] = (gate * _sigmoid(SWIGLU_ALPHA * gate) * (lin + 1.0)).astype(o_ref.dtype)


def _down_kernel(tile_ref, iblk_ref, oblk_ref, conv_ref, cexp_ref, cchunk_ref, slot_ref, bexp_ref,
                 a_ref, w32_ref, bd_ref, o_ref, wbuf_ref, wscale_ref):
    del iblk_ref, oblk_ref, bexp_ref
    step = pl.program_id(0)

    @pl.when(tile_ref[step] == FILL_STEP)
    def _():
        o_ref[...] = jnp.zeros_like(o_ref)


    @pl.when(conv_ref[step] == 1)
    def _():
        w8, scale = _quantize_columns(w32_ref[...])
        cslot = cexp_ref[step] % 2
        wbuf_ref[cslot, cchunk_ref[step]] = w8
        wscale_ref[cslot, cchunk_ref[step]] = scale

    @pl.when(tile_ref[step] >= 0)
    def _():
        slot = slot_ref[step]
        a8, a_scale = _quantize_rows(a_ref[...].astype(F32))
        for c in range(D_MODEL // DN_CHUNK):
            sl = slice(c * DN_CHUNK, (c + 1) * DN_CHUNK)
            y = jnp.dot(a8, wbuf_ref[slot, c], preferred_element_type=F32)
            o_ref[:, sl] = y * (a_scale * wscale_ref[slot, c]) + bd_ref[:, sl]


def _expert_ffn(tiles_per_expert, xs, wgu, bg, bl, wd, bd, tm):
    n_rows, d = xs.shape
    n_tiles = n_rows // tm
    f = wd.shape[1]
    rows_in = lambda w: pl.BlockSpec((tm, w), lambda s, tile, iblk, *_: (iblk[s], 0))
    rows_out = lambda w: pl.BlockSpec((tm, w), lambda s, tile, iblk, oblk, *_: (oblk[s], 0))
    bias = lambda w: pl.BlockSpec((None, 1, w), lambda s, *p: (p[7][s], 0, 0))

    n_chunks = 2 * f // GU_CHUNK
    assert n_chunks == d // DN_CHUNK
    n_steps = n_chunks + n_tiles + n_chunks * N_EXPERTS
    schedule = _work_schedule(tiles_per_expert, n_chunks, n_steps, n_tiles)
    act = pl.pallas_call(
        _gate_up_kernel,
        grid_spec=pltpu.PrefetchScalarGridSpec(
            num_scalar_prefetch=8,
            grid=(n_steps,),
            in_specs=[rows_in(d),
                      pl.BlockSpec((None, d, GU_CHUNK),
                                   lambda s, t, i, o, c, cexp, cchunk, *_: (cexp[s], 0, cchunk[s])),
                      bias(f), bias(f)],
            out_specs=rows_out(f),
            scratch_shapes=[pltpu.VMEM((2, 2 * n_chunks, d, GU_CHUNK // 2), F8),
                            pltpu.VMEM((2, 2 * n_chunks, 1, GU_CHUNK // 2), F32)],
        ),
        out_shape=jax.ShapeDtypeStruct((n_rows, f), BF16),
        compiler_params=_cparams(("arbitrary",)),
        name="expert_gate_up",
    )(*schedule, xs, wgu, bg, bl)

    return pl.pallas_call(
        _down_kernel,
        grid_spec=pltpu.PrefetchScalarGridSpec(
            num_scalar_prefetch=8,
            grid=(n_steps,),
            in_specs=[rows_in(f),
                      pl.BlockSpec((None, f, DN_CHUNK),
                                   lambda s, t, i, o, c, cexp, cchunk, *_: (cexp[s], 0, cchunk[s])),
                      bias(d)],
            out_specs=rows_out(d),
            scratch_shapes=[pltpu.VMEM((2, n_chunks, f, DN_CHUNK), F8),
                            pltpu.VMEM((2, n_chunks, 1, DN_CHUNK), F32)],
        ),
        out_shape=jax.ShapeDtypeStruct((n_rows, d), F32),
        compiler_params=_cparams(("arbitrary",)),
        name="expert_down",
    )(*schedule, act, wd, bd)


def _combine_kernel(dest_ref, y_ref, tw_ref, h1_ref, fw_ref, o_ref, buf_ref, sem, *, final_norm):
    i = pl.program_id(0)
    steps = pl.num_programs(0)
    tq = h1_ref.shape[0]
    n = steps * tq

    def issue(step, slot):
        def body(r, carry):
            for k in range(TOP_K):
                src = dest_ref[k * n + step * tq + r]
                pltpu.make_async_copy(y_ref.at[pl.ds(src, 1), :], buf_ref.at[slot, k, pl.ds(r, 1), :],
                                      sem.at[slot]).start()
            return carry
        lax.fori_loop(0, tq, body, 0, unroll=8)

    @pl.when(i == 0)
    def _():
        issue(0, 0)

    @pl.when(i + 1 < steps)
    def _():
        issue(i + 1, (i + 1) % 2)

    slot = i % 2
    for k in range(TOP_K):
        pltpu.make_async_copy(y_ref.at[pl.ds(0, tq), :], buf_ref.at[slot, k], sem.at[slot]).wait()

    acc = h1_ref[...]
    moe = tw_ref[:, 0:1] * buf_ref[slot, 0]
    for k in range(1, TOP_K):
        moe = moe + tw_ref[:, k:k + 1] * buf_ref[slot, k]
    acc = acc + moe
    if final_norm:
        acc = acc * lax.rsqrt(jnp.mean(acc * acc, axis=-1, keepdims=True) + RMS_EPS) * fw_ref[...]
    o_ref[...] = acc


def _combine(dest_flat, y, top_w_t, h1, final_w, final_norm):
    n, d = h1.shape
    tq = min(512, n)
    return pl.pallas_call(
        functools.partial(_combine_kernel, final_norm=final_norm),
        grid_spec=pltpu.PrefetchScalarGridSpec(
            num_scalar_prefetch=1,
            grid=(n // tq,),
            in_specs=[pl.BlockSpec(memory_space=pl.ANY),
                      pl.BlockSpec((tq, TOP_K), lambda i, dest: (i, 0)),
                      pl.BlockSpec((tq, d), lambda i, dest: (i, 0)),
                      pl.BlockSpec((1, d), lambda i, dest: (0, 0))],
            out_specs=pl.BlockSpec((tq, d), lambda i, dest: (i, 0)),
            scratch_shapes=[pltpu.VMEM((2, TOP_K, tq, d), F32), pltpu.SemaphoreType.DMA((2,))],
        ),
        out_shape=jax.ShapeDtypeStruct((n, d), F32),
        compiler_params=_cparams(("arbitrary",)),
        name="combine_rows",
    )(dest_flat, y, top_w_t, h1, final_w.reshape(1, d))


def _cast_kernel(w_ref, o_ref):
    o_ref[...] = w_ref[...].astype(o_ref.dtype)


def _cast_in_weights(w):
    d, width = w.shape
    tn = 2 * SWA_KV_W
    kv_blk = (2 * RET_QK_W + 2 * RET_V_W + SWA_Q_W) // tn
    n_blk = width // tn

    def source_block(j):
        return jnp.where(j < kv_blk, j, jnp.where(j < n_blk - 1, j + 1, kv_blk))

    return pl.pallas_call(
        _cast_kernel,
        grid=(n_blk,),
        in_specs=[pl.BlockSpec((d, tn), lambda j: (0, source_block(j)))],
        out_specs=pl.BlockSpec((d, tn), lambda j: (0, j)),
        out_shape=jax.ShapeDtypeStruct((d, width), BF16),
        compiler_params=_cparams(("parallel",)),
        name="cast_in_weights",
    )(w)


def _layer(h, tables, p, final_w, final_norm, expert_tile):
    b, s, d = h.shape
    n = b * s
    cr, sr, cs, sa, sb = tables
    x2 = h.reshape(n, d)
    proj = _in_projection(x2, p["attn_norm_w"], _cast_in_weights(p["w_in"]))
    proj3 = proj.reshape(b, s, IN_WIDTH)
    t3 = lambda t: t.reshape(b, s, LANES)
    ret = _retention(proj3, t3(cr), t3(sr))
    swa = _sliding_window(proj3, p["sinks"], t3(cs), t3(sa), t3(sb))
    h1, xn2, top_idx, top_w, rank, counts = _post_attention(
        ret.reshape(n, RET_V_W), swa.reshape(n, SWA_Q_W), proj, x2, p["gate_bias"],
        p["w_ret_out"].astype(BF16), p["w_swa_out"].astype(BF16), p["w_o"].astype(BF16),
        p["ffn_norm_w"], p["router_w"], p["router_b"])

    tm = expert_tile
    n_tiles = (n * TOP_K) // tm + N_EXPERTS
    cnt = counts[:, 0].astype(I32)
    padded = ((cnt + tm - 1) // tm) * tm
    ends = jnp.cumsum(padded)
    group_start = ends - padded
    n_active = (ends[-1] // tm).reshape(1)

    dest_flat = _dest_rows(top_idx, rank, group_start).reshape(TOP_K * n)
    xs = _dispatch(dest_flat, group_start + cnt, padded - cnt, n_active, xn2, n_tiles * tm, tm)

    bgu = p["b_gate_up"]
    y = _expert_ffn(padded // tm, xs, p["w_gate_up"], bgu[:, 0::2].reshape(N_EXPERTS, 1, D_FF),
                    bgu[:, 1::2].reshape(N_EXPERTS, 1, D_FF), p["w_down"], p["b_down"].reshape(N_EXPERTS, 1, d), tm)
    out = _combine(dest_flat, y, top_w.T, h1, final_w, final_norm)
    return out.reshape(b, s, d)


def kernel(x, positions, attn_norm_w, w_in, gate_bias, w_ret_out, w_swa_out, w_o, sinks, ffn_norm_w, router_w,
           router_b, w_gate_up, b_gate_up, w_down, b_down, final_norm_w):
    depth = w_in.shape[0]
    stacked = dict(attn_norm_w=attn_norm_w, w_in=w_in, gate_bias=gate_bias, w_ret_out=w_ret_out,
                   w_swa_out=w_swa_out, w_o=w_o, sinks=sinks, ffn_norm_w=ffn_norm_w, router_w=router_w,
                   router_b=router_b, w_gate_up=w_gate_up, b_gate_up=b_gate_up, w_down=w_down, b_down=b_down)
    tables = _rope_tables(positions)
    h = x
    for layer in range(depth):
        p = {name: w[layer] for name, w in stacked.items()}
        h = _layer(h, tables, p, final_norm_w, layer == depth - 1, expert_tile=256)
    return h
```

```python
import functools
import math

import jax
import jax.numpy as jnp
from jax import lax
from jax.experimental import pallas as pl
from jax.experimental.pallas import tpu as pltpu

F32 = jnp.float32
BF16 = jnp.bfloat16
I32 = jnp.int32

D_MODEL = 2048
RMS_EPS = 1e-5
ROPE_THETA = 10000.0
RET_HEADS = 8
RET_DK = 128
RET_DV = 256
RET_CHUNK = 128
SWA_Q_HEADS = 32
SWA_KV_HEADS = 4
SWA_HEAD_DIM = 64
SWA_WINDOW = 128
SWA_BLOCK = 128
N_EXPERTS = 32
TOP_K = 4
D_FF = 2048
SWIGLU_LIMIT = 7.0
SWIGLU_ALPHA = 1.702

RET_QK_W = RET_HEADS * RET_DK
RET_V_W = RET_HEADS * RET_DV
SWA_Q_W = SWA_Q_HEADS * SWA_HEAD_DIM
SWA_KV_W = SWA_KV_HEADS * SWA_HEAD_DIM
GATE_W = 2 * D_MODEL
IN_WIDTH = 2 * RET_QK_W + 2 * RET_V_W + SWA_Q_W + 2 * SWA_KV_W + GATE_W

LANES = 128
VMEM_LIMIT = 56 * 1024 * 1024
NEG_BIG = -1e30

_NT = (((1,), (1,)), ((), ()))
_TN = (((0,), (0,)), ((), ()))


def _cparams(sem):
    return pltpu.CompilerParams(dimension_semantics=sem, vmem_limit_bytes=VMEM_LIMIT)


def _sigmoid(x):
    return 1.0 / (1.0 + jnp.exp(-x))


def _rope_table_kernel(pos_ref, invr_ref, invs_ref, cr_ref, sr_ref, cs_ref, sa_ref, sb_ref):
    pos = pos_ref[...]
    lane = lax.broadcasted_iota(I32, (pos.shape[0], LANES), 1)
    ang_r = pos * invr_ref[...]
    cr_ref[...] = jnp.cos(ang_r)
    sr_ref[...] = jnp.where(lane < RET_DK // 2, -1.0, 1.0) * jnp.sin(ang_r)
    ang_s = pos * invs_ref[...]
    sin_s = jnp.sin(ang_s)
    first_half = (lane % SWA_HEAD_DIM) < SWA_HEAD_DIM // 2
    cs_ref[...] = jnp.cos(ang_s)
    sa_ref[...] = jnp.where(first_half, -sin_s, 0.0)
    sb_ref[...] = jnp.where(first_half, 0.0, sin_s)


def _rope_tables(positions):
    n = positions.size
    pos = positions.reshape(n, 1).astype(F32)
    inv_r = 1.0 / (ROPE_THETA ** (jnp.arange(0, RET_DK, 2, dtype=F32) / RET_DK))
    inv_s = 1.0 / (ROPE_THETA ** (jnp.arange(0, SWA_HEAD_DIM, 2, dtype=F32) / SWA_HEAD_DIM))
    inv_r = jnp.tile(inv_r, LANES // inv_r.size).reshape(1, LANES)
    inv_s = jnp.tile(inv_s, LANES // inv_s.size).reshape(1, LANES)
    t = min(2048, n)
    tab = pl.BlockSpec((t, LANES), lambda i: (i, 0))
    row = pl.BlockSpec((1, LANES), lambda i: (0, 0))
    return pl.pallas_call(
        _rope_table_kernel,
        grid=(n // t,),
        in_specs=[pl.BlockSpec((t, 1), lambda i: (i, 0)), row, row],
        out_specs=[tab] * 5,
        out_shape=[jax.ShapeDtypeStruct((n, LANES), F32)] * 5,
        compiler_params=_cparams(("parallel",)),
        name="rope_tables",
    )(pos, inv_r, inv_s)


def _inproj_kernel(x_ref, nw_ref, w_ref, o_ref, xn_ref):
    @pl.when(pl.program_id(1) == 0)
    def _():
        x = x_ref[...]
        ms = jnp.mean(x * x, axis=-1, keepdims=True)
        xn_ref[...] = (x * lax.rsqrt(ms + RMS_EPS) * nw_ref[...]).astype(BF16)

    o_ref[...] = jnp.dot(xn_ref[...], w_ref[...], preferred_element_type=F32).astype(o_ref.dtype)


def _in_projection(x2, norm_w, w_in_bf16):
    n, d = x2.shape
    width = w_in_bf16.shape[1]
    tm = min(1024, n)
    tn = 1280
    return pl.pallas_call(
        _inproj_kernel,
        grid=(n // tm, width // tn),
        in_specs=[pl.BlockSpec((tm, d), lambda i, j: (i, 0)),
                  pl.BlockSpec((1, d), lambda i, j: (0, 0)),
                  pl.BlockSpec((d, tn), lambda i, j: (0, j))],
        out_specs=pl.BlockSpec((tm, tn), lambda i, j: (i, j)),
        out_shape=jax.ShapeDtypeStruct((n, width), BF16),
        scratch_shapes=[pltpu.VMEM((tm, d), BF16)],
        compiler_params=_cparams(("parallel", "arbitrary")),
        name="in_projection",
    )(x2, norm_w.reshape(1, d), w_in_bf16)


def _retention_kernel(q_ref, k_ref, v_ref, g_ref, cr_ref, sr_ref, o_ref, st_ref, decay_ref, xi_ref, zeta_ref):
    c = RET_CHUNK

    @pl.when(jnp.logical_and(pl.program_id(0) == 0, pl.program_id(1) == 0))
    def _():
        row = lax.broadcasted_iota(I32, (c, c), 0)
        col = lax.broadcasted_iota(I32, (c, c), 1)
        rel = (row - col).astype(F32)
        n_row = row.astype(F32)
        scale = RET_DK ** -0.5
        for h in range(RET_HEADS):
            log_gamma = math.log1p(-(2.0 ** (-5.0 - h)))
            decay_ref[h] = jnp.where(rel >= 0, jnp.exp(log_gamma * jnp.maximum(rel, 0.0)), 0.0) * scale
            xi_ref[h] = jnp.exp(log_gamma * (n_row + 1.0))
            zeta_ref[h] = jnp.exp(log_gamma * (c - 1.0 - n_row)) * scale

    @pl.when(pl.program_id(1) == 0)
    def _():
        st_ref[...] = jnp.zeros_like(st_ref)

    cr = cr_ref[...]
    sr = sr_ref[...]
    for h in range(RET_HEADS):
        log_gamma = math.log1p(-(2.0 ** (-5.0 - h)))
        q = q_ref[:, h * RET_DK:(h + 1) * RET_DK].astype(F32)
        k = k_ref[:, h * RET_DK:(h + 1) * RET_DK].astype(F32)
        qr = q * cr + pltpu.roll(q, RET_DK // 2, 1) * sr
        kr = k * cr + pltpu.roll(k, RET_DK // 2, 1) * sr
        s = lax.dot_general(qr.astype(BF16), kr.astype(BF16), _NT, preferred_element_type=F32) * decay_ref[h]
        v = v_ref[:, h * RET_DV:(h + 1) * RET_DV]
        inner = jnp.dot(s.astype(BF16), v, preferred_element_type=F32)
        st = st_ref[h]
        cross = jnp.dot((qr * xi_ref[h]).astype(BF16), st.astype(BF16), preferred_element_type=F32)
        o = inner + cross
        u = lax.dot_general((kr * zeta_ref[h]).astype(BF16), v, _TN, preferred_element_type=F32)
        st_ref[h] = math.exp(log_gamma * c) * st + u
        o = o * lax.rsqrt(jnp.mean(o * o, axis=-1, keepdims=True) + RMS_EPS)
        g = g_ref[:, h * RET_DV:(h + 1) * RET_DV].astype(F32)
        o_ref[:, h * RET_DV:(h + 1) * RET_DV] = (g * _sigmoid(g) * o).astype(o_ref.dtype)


def _retention(proj3, cr3, sr3):
    b, s, _ = proj3.shape
    c = RET_CHUNK
    return pl.pallas_call(
        _retention_kernel,
        grid=(b, s // c),
        in_specs=[pl.BlockSpec((None, c, RET_QK_W), lambda i, j: (i, j, 0)),
                  pl.BlockSpec((None, c, RET_QK_W), lambda i, j: (i, j, 1)),
                  pl.BlockSpec((None, c, RET_V_W), lambda i, j: (i, j, 1)),
                  pl.BlockSpec((None, c, RET_V_W), lambda i, j: (i, j, 2)),
                  pl.BlockSpec((None, c, LANES), lambda i, j: (i, j, 0)),
                  pl.BlockSpec((None, c, LANES), lambda i, j: (i, j, 0))],
        out_specs=pl.BlockSpec((None, c, RET_V_W), lambda i, j: (i, j, 0)),
        out_shape=jax.ShapeDtypeStruct((b, s, RET_V_W), BF16),
        scratch_shapes=[pltpu.VMEM((RET_HEADS, RET_DK, RET_DV), F32)]
        + [pltpu.VMEM((RET_HEADS, c, c), F32)] * 3,
        compiler_params=_cparams(("arbitrary", "arbitrary")),
        name="retention",
    )(proj3, proj3, proj3, proj3, cr3, sr3)


def _swa_kernel(sinks_ref, q_ref, kv_ref, kvp_ref, cs_ref, sa_ref, sb_ref, csp_ref, sap_ref, sbp_ref, o_ref):
    blk = pl.program_id(1)
    wb = SWA_BLOCK
    hd = SWA_HEAD_DIM
    group = SWA_Q_HEADS // SWA_KV_HEADS

    def rope(x, cs, sa, sb):
        return x * cs + pltpu.roll(x, LANES - hd // 2, 1) * sa + pltpu.roll(x, hd // 2, 1) * sb

    def rope_keys(ref, tabs):
        return [rope(ref[:, j * LANES:(j + 1) * LANES].astype(F32), *tabs) for j in range(SWA_KV_W // LANES)]

    cur_tabs = (cs_ref[...], sa_ref[...], sb_ref[...])
    kc = rope_keys(kv_ref, cur_tabs)
    kp = rope_keys(kvp_ref, (csp_ref[...], sap_ref[...], sbp_ref[...]))
    vc = kv_ref[:, SWA_KV_W:2 * SWA_KV_W].astype(F32)
    vp = kvp_ref[:, SWA_KV_W:2 * SWA_KV_W].astype(F32)

    qr = [rope(q_ref[:, j * LANES:(j + 1) * LANES].astype(F32), *cur_tabs) * (hd ** -0.5)
          for j in range(SWA_Q_W // LANES)]

    def head_cols(slabs, h):
        return slabs[h // 2][:, (h % 2) * hd:(h % 2 + 1) * hd]

    cols = group * wb
    kj = lax.broadcasted_iota(I32, (2 * wb, cols), 0)
    qi = lax.broadcasted_iota(I32, (2 * wb, cols), 1) & (wb - 1)
    valid = (kj > qi) & (kj <= qi + SWA_WINDOW) & ((kj >= wb) | (blk > 0))

    for kh in range(SWA_KV_HEADS):
        kband = jnp.concatenate([head_cols(kp, kh), head_cols(kc, kh)], axis=0).astype(BF16)
        vband = jnp.concatenate([vp[:, kh * hd:(kh + 1) * hd], vc[:, kh * hd:(kh + 1) * hd]], axis=0).astype(BF16)
        heads = [kh * group + g for g in range(group)]
        qg = jnp.concatenate([head_cols(qr, h) for h in heads], axis=0).astype(BF16)
        sink = jnp.concatenate([jnp.full((1, wb), sinks_ref[h], F32) for h in heads], axis=1)
        s = lax.dot_general(kband, qg, _NT, preferred_element_type=F32)
        s = jnp.where(valid, s, NEG_BIG)
        m = jnp.maximum(jnp.max(s, axis=0, keepdims=True), sink)
        p = jnp.exp(s - m)
        denom = jnp.sum(p, axis=0, keepdims=True) + jnp.exp(sink - m)
        o = lax.dot_general((p * (1.0 / denom)).astype(BF16), vband, _TN, preferred_element_type=F32)
        for g in range(0, group, 2):
            pair = jnp.concatenate([o[g * wb:(g + 1) * wb], o[(g + 1) * wb:(g + 2) * wb]], axis=1)
            h = heads[g]
            o_ref[:, h * hd:(h + 2) * hd] = pair.astype(o_ref.dtype)


def _sliding_window(proj3, sinks, cs3, sa3, sb3):
    b, s, _ = proj3.shape
    wb = SWA_BLOCK
    q_blk = (2 * RET_QK_W + 2 * RET_V_W) // SWA_Q_W
    kv_blk = (2 * RET_QK_W + 2 * RET_V_W + SWA_Q_W + GATE_W) // (2 * SWA_KV_W)
    prev = lambda j: jnp.maximum(j - 1, 0)
    tab = pl.BlockSpec((None, wb, LANES), lambda i, j: (i, j, 0))
    tab_prev = pl.BlockSpec((None, wb, LANES), lambda i, j: (i, prev(j), 0))
    return pl.pallas_call(
        _swa_kernel,
        grid=(b, s // wb),
        in_specs=[pl.BlockSpec(memory_space=pltpu.SMEM),
                  pl.BlockSpec((None, wb, SWA_Q_W), lambda i, j: (i, j, q_blk)),
                  pl.BlockSpec((None, wb, 2 * SWA_KV_W), lambda i, j: (i, j, kv_blk)),
                  pl.BlockSpec((None, wb, 2 * SWA_KV_W), lambda i, j: (i, prev(j), kv_blk)),
                  tab, tab, tab, tab_prev, tab_prev, tab_prev],
        out_specs=pl.BlockSpec((None, wb, SWA_Q_W), lambda i, j: (i, j, 0)),
        out_shape=jax.ShapeDtypeStruct((b, s, SWA_Q_W), BF16),
        compiler_params=_cparams(("parallel", "parallel")),
        name="sliding_window",
    )(sinks, proj3, proj3, proj3, cs3, sa3, sb3, cs3, sa3, sb3)


def _post_kernel(ret_ref, swa_ref, gl_ref, x_ref, gb_ref, wro_ref, wso_ref, wo_ref, fw_ref, rwt_ref, rb_ref,
                 h1_ref, xn2_ref, idx_ref, tw_ref, rank_ref, cnt_ref, carry_ref):
    @pl.when(pl.program_id(0) == 0)
    def _():
        carry_ref[...] = jnp.zeros_like(carry_ref)

    d = D_MODEL
    rows = x_ref.shape[0]
    n_sub = 2
    for r in range(n_sub):
        rs = slice(r * rows // n_sub, (r + 1) * rows // n_sub)
        a = jnp.dot(ret_ref[rs, :], wro_ref[...], preferred_element_type=F32)
        b = jnp.dot(swa_ref[rs, :], wso_ref[...], preferred_element_type=F32)
        gl = gl_ref[rs, :].astype(F32) + gb_ref[...]
        mix = _sigmoid(gl[:, :d]) * a + _sigmoid(gl[:, d:]) * b
        h1 = x_ref[rs, :] + jnp.dot(mix.astype(BF16), wo_ref[...], preferred_element_type=F32)
        h1_ref[rs, :] = h1
        xn2_ref[rs, :] = h1 * lax.rsqrt(jnp.mean(h1 * h1, axis=-1, keepdims=True) + RMS_EPS) * fw_ref[...]
    xn2 = xn2_ref[...]

    xh = xn2.astype(BF16)
    xl = (xn2 - xh.astype(F32)).astype(BF16)
    rw = rwt_ref[...]
    rh = rw.astype(BF16)
    rl = (rw - rh.astype(F32)).astype(BF16)
    logits = (lax.dot_general(rh, xh, _NT, preferred_element_type=F32)
              + lax.dot_general(rh, xl, _NT, preferred_element_type=F32)
              + lax.dot_general(rl, xh, _NT, preferred_element_type=F32)) + rb_ref[...]
    tm = logits.shape[1]
    e_iota = lax.broadcasted_iota(I32, (N_EXPERTS, tm), 0)
    vals, idxs = [], []
    for _ in range(TOP_K):
        m = jnp.max(logits, axis=0, keepdims=True)
        ix = jnp.min(jnp.where(logits == m, e_iota, N_EXPERTS), axis=0, keepdims=True)
        vals.append(m)
        idxs.append(ix)
        logits = jnp.where(e_iota == ix, -jnp.inf, logits)
    ex = [jnp.exp(v - vals[0]) for v in vals]
    den = ex[0] + ex[1] + ex[2] + ex[3]
    tw_ref[...] = jnp.concatenate([e / den for e in ex], axis=0)
    idx_ref[...] = jnp.concatenate(idxs, axis=0)

    onehot = jnp.zeros((N_EXPERTS, tm), F32)
    for ix in idxs:
        onehot = onehot + (e_iota == ix).astype(F32)
    earlier = (lax.broadcasted_iota(I32, (tm, tm), 0) < lax.broadcasted_iota(I32, (tm, tm), 1)).astype(BF16)
    prefix = jnp.dot(onehot.astype(BF16), earlier, preferred_element_type=F32) + carry_ref[:, 0:1]
    ranks = [jnp.sum(jnp.where(e_iota == ix, prefix, 0.0), axis=0, keepdims=True) for ix in idxs]
    rank_ref[...] = jnp.concatenate(ranks, axis=0).astype(I32)
    carry = carry_ref[...] + jnp.sum(onehot, axis=1, keepdims=True)
    carry_ref[...] = carry
    cnt_ref[...] = carry


def _post_attention(ret2, swa2, proj2, x2, gate_bias, wro, wso, wo, ffn_w, router_w, router_b):
    n, d = x2.shape
    tm = min(256, n)
    gl_blk = (2 * RET_QK_W + 2 * RET_V_W + SWA_Q_W) // GATE_W
    const = lambda shape: pl.BlockSpec(shape, lambda i: (0,) * len(shape), pipeline_mode=pl.Buffered(1))
    rowblk = lambda w: pl.BlockSpec((tm, w), lambda i: (i, 0))
    tokrow = pl.BlockSpec((TOP_K, tm), lambda i: (0, i))
    return pl.pallas_call(
        _post_kernel,
        grid=(n // tm,),
        in_specs=[rowblk(RET_V_W), rowblk(SWA_Q_W),
                  pl.BlockSpec((tm, GATE_W), lambda i: (i, gl_blk)),
                  rowblk(d), const((1, GATE_W)),
                  const((RET_V_W, d)), const((SWA_Q_W, d)), const((d, d)),
                  const((1, d)), const((N_EXPERTS, d)), const((N_EXPERTS, 1))],
        out_specs=[rowblk(d), rowblk(d), tokrow, tokrow, tokrow,
                   pl.BlockSpec((N_EXPERTS, LANES), lambda i: (0, 0))],
        out_shape=[jax.ShapeDtypeStruct((n, d), F32), jax.ShapeDtypeStruct((n, d), F32),
                   jax.ShapeDtypeStruct((TOP_K, n), I32), jax.ShapeDtypeStruct((TOP_K, n), F32),
                   jax.ShapeDtypeStruct((TOP_K, n), I32), jax.ShapeDtypeStruct((N_EXPERTS, LANES), F32)],
        scratch_shapes=[pltpu.VMEM((N_EXPERTS, LANES), F32)],
        compiler_params=_cparams(("arbitrary",)),
        name="post_attention",
    )(ret2, swa2, proj2, x2, gate_bias.reshape(1, GATE_W), wro, wso, wo, ffn_w.reshape(1, d),
      router_w.T, router_b.reshape(N_EXPERTS, 1))


def _dest_kernel(idx_ref, rank_ref, gs_ref, o_ref):
    t = idx_ref.shape[1]
    e_iota = lax.broadcasted_iota(I32, (N_EXPERTS, t), 0)
    gs = gs_ref[...]
    rows = []
    for k in range(TOP_K):
        start = jnp.sum(jnp.where(e_iota == idx_ref[k:k + 1, :], gs, 0), axis=0, keepdims=True)
        rows.append(start + rank_ref[k:k + 1, :])
    o_ref[...] = jnp.concatenate(rows, axis=0)


def _dest_rows(top_idx, rank, group_start):
    n = top_idx.shape[1]
    t = min(2048, n)
    blk = pl.BlockSpec((TOP_K, t), lambda i: (0, i))
    return pl.pallas_call(
        _dest_kernel,
        grid=(n // t,),
        in_specs=[blk, blk, pl.BlockSpec((N_EXPERTS, 1), lambda i: (0, 0))],
        out_specs=blk,
        out_shape=jax.ShapeDtypeStruct((TOP_K, n), I32),
        compiler_params=_cparams(("parallel",)),
        name="dest_rows",
    )(top_idx, rank, group_start.reshape(N_EXPERTS, 1))


def _row_copy(src, s, dst, d, sem):
    return pltpu.make_async_copy(src.at[pl.ds(s, 1), :], dst.at[pl.ds(d, 1), :], sem)


def _dispatch_kernel(dest_ref, pad_lo_ref, pad_n_ref, na_ref, x_ref, o_ref, zero_ref, sem, zsem, *, tm):
    i = pl.program_id(0)
    tq = x_ref.shape[0]
    n = pl.num_programs(0) * tq
    n_tiles = o_ref.shape[0] // tm

    @pl.when(i == 0)
    def _():
        zero_ref[...] = jnp.zeros_like(zero_ref)

        def pad_rows(e, total):
            def one(r, carry):
                _row_copy(zero_ref, 0, o_ref, pad_lo_ref[e] + r, zsem).start()
                return carry
            lax.fori_loop(0, pad_n_ref[e], one, 0)
            return total + pad_n_ref[e]

        n_pad = lax.fori_loop(0, N_EXPERTS, pad_rows, 0)

        def tile_copy(t):
            return pltpu.make_async_copy(zero_ref, o_ref.at[pl.ds(t * tm, tm), :], zsem)

        def start_tile(t, carry):
            tile_copy(t).start()
            return carry

        def drain_tile(t, carry):
            tile_copy(0).wait()
            return carry

        def drain_row(r, carry):
            _row_copy(zero_ref, 0, o_ref, 0, zsem).wait()
            return carry

        lax.fori_loop(na_ref[0], n_tiles, start_tile, 0)
        lax.fori_loop(0, n_pad, drain_row, 0)
        lax.fori_loop(na_ref[0], n_tiles, drain_tile, 0)

    def issue(r, carry):
        for k in range(TOP_K):
            _row_copy(x_ref, r, o_ref, dest_ref[k * n + i * tq + r], sem).start(priority=k % 2)
        return carry

    lax.fori_loop(0, tq, issue, 0, unroll=8)
    for _ in range(TOP_K):
        pltpu.make_async_copy(x_ref, o_ref.at[pl.ds(0, tq), :], sem).wait()


def _dispatch(dest_flat, pad_lo, pad_n, n_active, xn2, n_rows, tm):
    n, d = xn2.shape
    tq = min(512, n)
    return pl.pallas_call(
        functools.partial(_dispatch_kernel, tm=tm),
        grid_spec=pltpu.PrefetchScalarGridSpec(
            num_scalar_prefetch=4,
            grid=(n // tq,),
            in_specs=[pl.BlockSpec((tq, d), lambda i, *_: (i, 0))],
            out_specs=pl.BlockSpec(memory_space=pl.ANY),
            scratch_shapes=[pltpu.VMEM((tm, d), F32), pltpu.SemaphoreType.DMA(()), pltpu.SemaphoreType.DMA(())],
        ),
        out_shape=jax.ShapeDtypeStruct((n_rows, d), F32),
        compiler_params=_cparams(("arbitrary",)),
        name="dispatch_rows",
    )(dest_flat, pad_lo, pad_n, n_active, xn2)


GU_CHUNK = 512
DN_CHUNK = 256


F8 = jnp.float8_e4m3fn
F8_MAX = 448.0
TINY = 1e-30


def _quantize_columns(w):
    amax = jnp.maximum(jnp.max(jnp.abs(w), axis=0, keepdims=True), TINY)
    return (w * (F8_MAX / amax)).astype(F8), amax * (1.0 / F8_MAX)


def _quantize_rows(x):
    amax = jnp.maximum(jnp.max(jnp.abs(x), axis=1, keepdims=True), TINY)
    return (x * (F8_MAX / amax)).astype(F8), amax * (1.0 / F8_MAX)


IDLE_STEP = -1
FILL_STEP = -2


def _work_schedule(tiles_per_expert, n_chunks, n_steps, n_tiles):
    e_ids = jnp.arange(N_EXPERTS, dtype=I32)
    steps_e = jnp.where(e_ids < N_EXPERTS - 1, jnp.maximum(tiles_per_expert, n_chunks), tiles_per_expert)
    end_e = n_chunks + jnp.cumsum(steps_e)
    start_e = end_e - steps_e
    first_tile_e = jnp.cumsum(tiles_per_expert) - tiles_per_expert
    w = jnp.arange(n_steps, dtype=I32)[:, None]
    prologue = w[:, 0] < n_chunks
    in_e = jnp.logical_and(w >= start_e[None, :], w < end_e[None, :])
    local_e = w - start_e[None, :]
    tile_e = jnp.logical_and(in_e, local_e < tiles_per_expert[None, :])
    conv_e = jnp.logical_and(jnp.logical_and(in_e, local_e < n_chunks), e_ids[None, :] < N_EXPERTS - 1)
    has_tile = jnp.any(tile_e, axis=1)
    has_conv = jnp.logical_or(prologue, jnp.any(conv_e, axis=1))
    tile_id = jnp.sum(jnp.where(tile_e, first_tile_e[None, :] + local_e, 0), axis=1)
    cur_expert = jnp.sum(jnp.where(in_e, e_ids[None, :], 0), axis=1)
    cur_expert = jnp.where(w[:, 0] >= end_e[N_EXPERTS - 1], N_EXPERTS - 1, cur_expert)
    n_used = jnp.sum(tiles_per_expert)
    spare = w[:, 0] - end_e[N_EXPERTS - 1]
    fill = jnp.logical_and(spare >= 0, n_used + spare < n_tiles)
    tiles_done = jnp.sum(jnp.clip(w + 1 - start_e[None, :], 0, tiles_per_expert[None, :]), axis=1)
    fills_done = jnp.clip(spare + 1, 0, n_tiles - n_used)
    convs_done = jnp.clip(w[:, 0] + 1, 0, n_chunks) + jnp.sum(
        jnp.where(e_ids[None, :] < N_EXPERTS - 1, jnp.clip(w + 1 - start_e[None, :], 0, n_chunks), 0), axis=1)
    in_blk = jnp.maximum(tiles_done - 1, 0)
    out_blk = jnp.maximum(tiles_done + fills_done - 1, 0)
    conv_lin = jnp.maximum(convs_done - 1, 0)
    tile = jnp.where(has_tile, tile_id, jnp.where(fill, FILL_STEP, IDLE_STEP))
    return (tile.astype(I32), in_blk.astype(I32), out_blk.astype(I32), has_conv.astype(I32),
            (conv_lin // n_chunks).astype(I32), (conv_lin % n_chunks).astype(I32), (cur_expert % 2).astype(I32),
            cur_expert.astype(I32))


def _gate_up_kernel(tile_ref, iblk_ref, oblk_ref, conv_ref, cexp_ref, cchunk_ref, slot_ref, bexp_ref,
                    x_ref, w32_ref, bg_ref, bl_ref, o_ref, wbuf_ref, wscale_ref):
    del iblk_ref, oblk_ref, bexp_ref
    step = pl.program_id(0)

    @pl.when(tile_ref[step] == FILL_STEP)
    def _():
        o_ref[...] = jnp.zeros_like(o_ref)

    half = GU_CHUNK // 2
    n_slabs = 2 * D_FF // GU_CHUNK

    @pl.when(conv_ref[step] == 1)
    def _():
        src = lax.broadcasted_iota(I32, (half, half), 0)
        dst = lax.broadcasted_iota(I32, (half, half), 1)
        perm = (src == jnp.where(dst < half // 2, 2 * dst, 2 * (dst - half // 2) + 1)).astype(F32)
        w8, scale = _quantize_columns(w32_ref[...])
        rows = jnp.broadcast_to(scale, (8, GU_CHUNK))
        s_hi = rows.astype(BF16)
        rest = rows - s_hi.astype(F32)
        s_mid = rest.astype(BF16)
        s_lo = (rest - s_mid.astype(F32)).astype(BF16)
        w_parts, s_parts = [], []
        for j in range(2):
            cols = slice(j * half, (j + 1) * half)
            w_parts.append(jnp.dot(w8[:, cols], perm.astype(F8), preferred_element_type=F32).astype(F8))
            s_parts.append(sum(jnp.dot(piece[:, cols], perm.astype(BF16), preferred_element_type=F32)
                               for piece in (s_hi, s_mid, s_lo))[0:1])
        cslot = cexp_ref[step] % 2
        c = cchunk_ref[step]
        for slab, lo in ((c, 0), (n_slabs + c, half // 2)):
            wbuf_ref[cslot, slab] = jnp.concatenate([p[:, lo:lo + half // 2] for p in w_parts], axis=1)
            wscale_ref[cslot, slab] = jnp.concatenate([p[:, lo:lo + half // 2] for p in s_parts], axis=1)

    @pl.when(tile_ref[step] >= 0)
    def _():
        slot = slot_ref[step]
        x, x_scale = _quantize_rows(x_ref[...])
        per = 2
        for c in range(0, n_slabs, per):
            sl = slice(c * half, (c + per) * half)
            wg = jnp.concatenate([wbuf_ref[slot, c + j] for j in range(per)], axis=1)
            wl = jnp.concatenate([wbuf_ref[slot, n_slabs + c + j] for j in range(per)], axis=1)
            sg = jnp.concatenate([wscale_ref[slot, c + j] for j in range(per)], axis=1)
            sl_scale = jnp.concatenate([wscale_ref[slot, n_slabs + c + j] for j in range(per)], axis=1)
            gate = jnp.dot(x, wg, preferred_element_type=F32) * (x_scale * sg) + bg_ref[:, sl]
            lin = jnp.dot(x, wl, preferred_element_type=F32) * (x_scale * sl_scale) + bl_ref[:, sl]
            gate = jnp.minimum(gate, SWIGLU_LIMIT)
            lin = jnp.clip(lin, -SWIGLU_LIMIT, SWIGLU_LIMIT)
            o_ref[:, sl] = (gate * _sigmoid(SWIGLU_ALPHA * gate) * (lin + 1.0)).astype(o_ref.dtype)


def _down_kernel(tile_ref, iblk_ref, oblk_ref, conv_ref, cexp_ref, cchunk_ref, slot_ref, bexp_ref,
                 a_ref, w32_ref, bd_ref, o_ref, wbuf_ref, wscale_ref):
    del iblk_ref, oblk_ref, bexp_ref
    step = pl.program_id(0)

    @pl.when(tile_ref[step] == FILL_STEP)
    def _():
        o_ref[...] = jnp.zeros_like(o_ref)


    @pl.when(conv_ref[step] == 1)
    def _():
        w8, scale = _quantize_columns(w32_ref[...])
        cslot = cexp_ref[step] % 2
        wbuf_ref[cslot, cchunk_ref[step]] = w8
        wscale_ref[cslot, cchunk_ref[step]] = scale

    @pl.when(tile_ref[step] >= 0)
    def _():
        slot = slot_ref[step]
        a8, a_scale = _quantize_rows(a_ref[...].astype(F32))
        for c in range(D_MODEL // DN_CHUNK):
            sl = slice(c * DN_CHUNK, (c + 1) * DN_CHUNK)
            y = jnp.dot(a8, wbuf_ref[slot, c], preferred_element_type=F32)
            o_ref[:, sl] = y * (a_scale * wscale_ref[slot, c]) + bd_ref[:, sl]


def _expert_ffn(tiles_per_expert, xs, wgu, bg, bl, wd, bd, tm):
    n_rows, d = xs.shape
    n_tiles = n_rows // tm
    f = wd.shape[1]
    rows_in = lambda w: pl.BlockSpec((tm, w), lambda s, tile, iblk, *_: (iblk[s], 0))
    rows_out = lambda w: pl.BlockSpec((tm, w), lambda s, tile, iblk, oblk, *_: (oblk[s], 0))
    bias = lambda w: pl.BlockSpec((None, 1, w), lambda s, *p: (p[7][s], 0, 0))

    n_chunks = 2 * f // GU_CHUNK
    assert n_chunks == d // DN_CHUNK
    n_steps = n_chunks + n_tiles + n_chunks * N_EXPERTS
    schedule = _work_schedule(tiles_per_expert, n_chunks, n_steps, n_tiles)
    act = pl.pallas_call(
        _gate_up_kernel,
        grid_spec=pltpu.PrefetchScalarGridSpec(
            num_scalar_prefetch=8,
            grid=(n_steps,),
            in_specs=[rows_in(d),
                      pl.BlockSpec((None, d, GU_CHUNK),
                                   lambda s, t, i, o, c, cexp, cchunk, *_: (cexp[s], 0, cchunk[s])),
                      bias(f), bias(f)],
            out_specs=rows_out(f),
            scratch_shapes=[pltpu.VMEM((2, 2 * n_chunks, d, GU_CHUNK // 2), F8),
                            pltpu.VMEM((2, 2 * n_chunks, 1, GU_CHUNK // 2), F32)],
        ),
        out_shape=jax.ShapeDtypeStruct((n_rows, f), BF16),
        compiler_params=_cparams(("arbitrary",)),
        name="expert_gate_up",
    )(*schedule, xs, wgu, bg, bl)

    return pl.pallas_call(
        _down_kernel,
        grid_spec=pltpu.PrefetchScalarGridSpec(
            num_scalar_prefetch=8,
            grid=(n_steps,),
            in_specs=[rows_in(f),
                      pl.BlockSpec((None, f, DN_CHUNK),
                                   lambda s, t, i, o, c, cexp, cchunk, *_: (cexp[s], 0, cchunk[s])),
                      bias(d)],
            out_specs=rows_out(d),
            scratch_shapes=[pltpu.VMEM((2, n_chunks, f, DN_CHUNK), F8),
                            pltpu.VMEM((2, n_chunks, 1, DN_CHUNK), F32)],
        ),
        out_shape=jax.ShapeDtypeStruct((n_rows, d), F32),
        compiler_params=_cparams(("arbitrary",)),
        name="expert_down",
    )(*schedule, act, wd, bd)


def _combine_kernel(dest_ref, y_ref, tw_ref, h1_ref, fw_ref, o_ref, buf_ref, sem, *, final_norm):
    i = pl.program_id(0)
    steps = pl.num_programs(0)
    tq = h1_ref.shape[0]
    n = steps * tq

    def issue(step, slot):
        def body(r, carry):
            for k in range(TOP_K):
                src = dest_ref[k * n + step * tq + r]
                pltpu.make_async_copy(y_ref.at[pl.ds(src, 1), :], buf_ref.at[slot, k, pl.ds(r, 1), :],
                                      sem.at[slot]).start(priority=k % 2)
            return carry
        lax.fori_loop(0, tq, body, 0, unroll=8)

    @pl.when(i == 0)
    def _():
        issue(0, 0)

    @pl.when(i + 1 < steps)
    def _():
        issue(i + 1, (i + 1) % 2)

    slot = i % 2
    for k in range(TOP_K):
        pltpu.make_async_copy(y_ref.at[pl.ds(0, tq), :], buf_ref.at[slot, k], sem.at[slot]).wait()

    acc = h1_ref[...]
    moe = tw_ref[:, 0:1] * buf_ref[slot, 0]
    for k in range(1, TOP_K):
        moe = moe + tw_ref[:, k:k + 1] * buf_ref[slot, k]
    acc = acc + moe
    if final_norm:
        acc = acc * lax.rsqrt(jnp.mean(acc * acc, axis=-1, keepdims=True) + RMS_EPS) * fw_ref[...]
    o_ref[...] = acc


def _combine(dest_flat, y, top_w_t, h1, final_w, final_norm):
    n, d = h1.shape
    tq = min(256, n)
    return pl.pallas_call(
        functools.partial(_combine_kernel, final_norm=final_norm),
        grid_spec=pltpu.PrefetchScalarGridSpec(
            num_scalar_prefetch=1,
            grid=(n // tq,),
            in_specs=[pl.BlockSpec(memory_space=pl.ANY),
                      pl.BlockSpec((tq, TOP_K), lambda i, dest: (i, 0)),
                      pl.BlockSpec((tq, d), lambda i, dest: (i, 0)),
                      pl.BlockSpec((1, d), lambda i, dest: (0, 0))],
            out_specs=pl.BlockSpec((tq, d), lambda i, dest: (i, 0)),
            scratch_shapes=[pltpu.VMEM((2, TOP_K, tq, d), F32), pltpu.SemaphoreType.DMA((2,))],
        ),
        out_shape=jax.ShapeDtypeStruct((n, d), F32),
        compiler_params=_cparams(("arbitrary",)),
        name="combine_rows",
    )(dest_flat, y, top_w_t, h1, final_w.reshape(1, d))


def _cast_kernel(w_ref, o_ref):
    o_ref[...] = w_ref[...].astype(o_ref.dtype)


def _cast_in_weights(w):
    d, width = w.shape
    tn = 2 * SWA_KV_W
    kv_blk = (2 * RET_QK_W + 2 * RET_V_W + SWA_Q_W) // tn
    n_blk = width // tn

    def source_block(j):
        return jnp.where(j < kv_blk, j, jnp.where(j < n_blk - 1, j + 1, kv_blk))

    return pl.pallas_call(
        _cast_kernel,
        grid=(n_blk,),
        in_specs=[pl.BlockSpec((d, tn), lambda j: (0, source_block(j)))],
        out_specs=pl.BlockSpec((d, tn), lambda j: (0, j)),
        out_shape=jax.ShapeDtypeStruct((d, width), BF16),
        compiler_params=_cparams(("parallel",)),
        name="cast_in_weights",
    )(w)


def _layer(h, tables, p, final_w, final_norm, expert_tile):
    b, s, d = h.shape
    n = b * s
    cr, sr, cs, sa, sb = tables
    x2 = h.reshape(n, d)
    proj = _in_projection(x2, p["attn_norm_w"], _cast_in_weights(p["w_in"]))
    proj3 = proj.reshape(b, s, IN_WIDTH)
    t3 = lambda t: t.reshape(b, s, LANES)
    ret = _retention(proj3, t3(cr), t3(sr))
    swa = _sliding_window(proj3, p["sinks"], t3(cs), t3(sa), t3(sb))
    h1, xn2, top_idx, top_w, rank, counts = _post_attention(
        ret.reshape(n, RET_V_W), swa.reshape(n, SWA_Q_W), proj, x2, p["gate_bias"],
        p["w_ret_out"].astype(BF16), p["w_swa_out"].astype(BF16), p["w_o"].astype(BF16),
        p["ffn_norm_w"], p["router_w"], p["router_b"])

    tm = expert_tile
    n_tiles = (n * TOP_K) // tm + N_EXPERTS
    cnt = counts[:, 0].astype(I32)
    padded = ((cnt + tm - 1) // tm) * tm
    ends = jnp.cumsum(padded)
    group_start = ends - padded
    n_active = (ends[-1] // tm).reshape(1)

    dest_flat = _dest_rows(top_idx, rank, group_start).reshape(TOP_K * n)
    xs = _dispatch(dest_flat, group_start + cnt, padded - cnt, n_active, xn2, n_tiles * tm, tm)

    bgu = p["b_gate_up"]
    y = _expert_ffn(padded // tm, xs, p["w_gate_up"], bgu[:, 0::2].reshape(N_EXPERTS, 1, D_FF),
                    bgu[:, 1::2].reshape(N_EXPERTS, 1, D_FF), p["w_down"], p["b_down"].reshape(N_EXPERTS, 1, d), tm)
    out = _combine(dest_flat, y, top_w.T, h1, final_w, final_norm)
    return out.reshape(b, s, d)


def kernel(x, positions, attn_norm_w, w_in, gate_bias, w_ret_out, w_swa_out, w_o, sinks, ffn_norm_w, router_w,
           router_b, w_gate_up, b_gate_up, w_down, b_down, final_norm_w):
    depth = w_in.shape[0]
    stacked = dict(attn_norm_w=attn_norm_w, w_in=w_in, gate_bias=gate_bias, w_ret_out=w_ret_out,
                   w_swa_out=w_swa_out, w_o=w_o, sinks=sinks, ffn_norm_w=ffn_norm_w, router_w=router_w,
                   router_b=router_b, w_gate_up=w_gate_up, b_gate_up=b_gate_up, w_down=w_down, b_down=b_down)
    tables = _rope_tables(positions)
    h = x
    for layer in range(depth):
        p = {name: w[layer] for name, w in stacked.items()}
        h = _layer(h, tables, p, final_norm_w, layer == depth - 1, expert_tile=256)
    return h
```
